```python
import math
import jax, jax.numpy as jnp
from jax import lax
import numpy as np

D_MODEL = 1024
BATCH = 4
SEQ = 4096
DEPTH = 4
DEC_BATCH = 128
DEC_SEQ = 1
PAST_LEN = 2048
PAGE_SIZE = 128

HEAD_DIM = 64
FOX_HEADS = 4
NSA_HEADS = 4
NSA_BLOCK = 64
NSA_TOPK = 16
NSA_WINDOW = 512
DIFF_HEADS = 4
FOX_W = FOX_HEADS * HEAD_DIM
NSA_W = NSA_HEADS * HEAD_DIM
DIFF_W = DIFF_HEADS * 2 * HEAD_DIM
D_FF = 4 * D_MODEL
PLE_DIM = 256
ROPE_THETA = 10000.0
QUERY_BLOCK = 128
RMS_EPS = 1e-6
NEG_INF = -1e30
FORGET_BIAS = 3.0
IN_SIZES = (FOX_W, FOX_W, FOX_W, FOX_HEADS,
            NSA_W, HEAD_DIM, HEAD_DIM, HEAD_DIM, HEAD_DIM, HEAD_DIM, HEAD_DIM, 3 * NSA_HEADS,
            DIFF_W, DIFF_W, DIFF_W)
D_IN = sum(IN_SIZES)

kernel_name = 'fox_nsa_diff_hybrid_step'


def rmsnorm(x, g):
    xf = x.astype(jnp.float32)
    y = xf * lax.rsqrt(jnp.mean(xf * xf, axis=-1, keepdims=True) + RMS_EPS)
    return (y * g.astype(jnp.float32)).astype(x.dtype)


def rope(x, pos):
    half = x.shape[-1] // 2
    inv = ROPE_THETA ** (-jnp.arange(half, dtype=jnp.float32) / half)
    ang = pos.astype(jnp.float32)[:, None] * inv[None, :]
    cos = jnp.cos(ang)[None, :, None, :]
    sin = jnp.sin(ang)[None, :, None, :]
    xf = x.astype(jnp.float32)
    x1, x2 = xf[..., :half], xf[..., half:]
    return jnp.concatenate([x1 * cos - x2 * sin, x2 * cos + x1 * sin], axis=-1).astype(x.dtype)


def masked_softmax(s, mask):
    return jax.nn.softmax(jnp.where(mask, s, NEG_INF), axis=-1) * mask


def map_query_blocks(fn, qs, pos_q):
    t = pos_q.shape[0]
    if t <= QUERY_BLOCK or t % QUERY_BLOCK:
        return fn(qs, pos_q)
    nb = t // QUERY_BLOCK
    split = lambda a: jnp.moveaxis(a.reshape((a.shape[0], nb, QUERY_BLOCK) + a.shape[2:]), 1, 0)
    outs = lax.map(lambda args: fn(args[0], args[1]),
                   (tuple(split(a) for a in qs), pos_q.reshape(nb, QUERY_BLOCK)))
    return tuple(jnp.moveaxis(o, 0, 1).reshape((o.shape[1], t) + o.shape[3:]) for o in outs)


def gather_pages(pool, page_table):
    g = pool[page_table]
    return g.reshape((g.shape[0], g.shape[1] * g.shape[2]) + g.shape[3:])


def fox_attention(q, k, v, logf, pos_q, pos_k):
    t = q.shape[1]
    c = jnp.cumsum(logf.astype(jnp.float32), axis=1)
    c_k = jnp.swapaxes(c, 1, 2)
    c_q = c[:, c.shape[1] - t:]
    scale = HEAD_DIM ** -0.5

    def blk(qs, pq):
        qb, cq = qs
        s = jnp.einsum('bqhd,bkhd->bhqk', qb, k).astype(jnp.float32) * scale
        s = s + jnp.swapaxes(cq, 1, 2)[..., None] - c_k[:, :, None, :]
        p = masked_softmax(s, (pos_k[None, :] <= pq[:, None])[None, None])
        return (jnp.einsum('bhqk,bkhd->bqhd', p.astype(v.dtype), v),)

    return map_query_blocks(blk, (q, c_q), pos_q)[0]


def diff_attention(q, k, v, lam, pos_q, pos_k):
    k1, k2 = k[..., :HEAD_DIM], k[..., HEAD_DIM:]
    scale = HEAD_DIM ** -0.5

    def blk(qs, pq):
        (qb,) = qs
        mask = (pos_k[None, :] <= pq[:, None])[None, None]
        s1 = jnp.einsum('bqhd,bkhd->bhqk', qb[..., :HEAD_DIM], k1).astype(jnp.float32) * scale
        s2 = jnp.einsum('bqhd,bkhd->bhqk', qb[..., HEAD_DIM:], k2).astype(jnp.float32) * scale
        a = masked_softmax(s1, mask) - lam * masked_softmax(s2, mask)
        return (jnp.einsum('bhqk,bkhd->bqhd', a.astype(v.dtype), v),)

    return map_query_blocks(blk, (q,), pos_q)[0]


def nsa_cmp_sel(q, kc, vc, ks, vs, pe, phi, pos_q):
    b, total, d = kc.shape
    n_cmp = total // NSA_BLOCK
    n_blk = -(-total // NSA_BLOCK)
    n_sel = min(NSA_TOPK, n_blk)
    scale = HEAD_DIM ** -0.5

    def compress(rows, pe_j, phi_j):
        blocks = rows[:, :n_cmp * NSA_BLOCK].reshape(b, n_cmp, NSA_BLOCK, d)
        return jnp.einsum('bnd,de->bne', jnp.mean(blocks + pe_j, axis=2), phi_j)

    kcb = compress(kc, pe[0], phi[0])
    vcb = compress(vc, pe[1], phi[1])
    pad = n_blk * NSA_BLOCK - total
    ksb = jnp.pad(ks, ((0, 0), (0, pad), (0, 0))).reshape(b, n_blk, NSA_BLOCK, d)
    vsb = jnp.pad(vs, ((0, 0), (0, pad), (0, 0))).reshape(b, n_blk, NSA_BLOCK, d)
    cmp_end = (jnp.arange(n_cmp) + 1) * NSA_BLOCK - 1
    blk_ids = jnp.arange(n_blk)
    offs = jnp.arange(NSA_BLOCK)
    gather = jax.vmap(lambda blocks, idx: blocks[idx])

    def blk(qs, pq):
        (qb,) = qs
        tq = pq.shape[0]
        s_c = jnp.einsum('bqhd,bnd->bqhn', qb, kcb).astype(jnp.float32) * scale
        p_c = masked_softmax(s_c, (cmp_end[None, :] <= pq[:, None])[None, :, None, :])
        o_c = jnp.einsum('bqhn,bnd->bqhd', p_c.astype(vcb.dtype), vcb)
        imp = jnp.pad(jnp.sum(p_c, axis=2), ((0, 0), (0, 0), (0, n_blk - n_cmp)),
                      constant_values=-jnp.inf)
        cur = (pq // NSA_BLOCK)[:, None]
        forced = (blk_ids == 0) | (blk_ids == cur) | (blk_ids == cur - 1)
        score = jnp.where(forced, jnp.inf, jnp.where(blk_ids <= cur, imp, -jnp.inf))
        _, idx = lax.top_k(score, n_sel)
        kg = gather(ksb, idx)
        vg = gather(vsb, idx)
        kpos = idx[..., None] * NSA_BLOCK + offs
        ok = (idx <= cur[None])[..., None] & (kpos <= pq[None, :, None, None])
        s_s = jnp.einsum('bqhd,bqkjd->bqhkj', qb, kg).astype(jnp.float32) * scale
        s_s = s_s.reshape(b, tq, NSA_HEADS, n_sel * NSA_BLOCK)
        p_s = masked_softmax(s_s, ok.reshape(b, tq, 1, n_sel * NSA_BLOCK))
        p_s = p_s.reshape(b, tq, NSA_HEADS, n_sel, NSA_BLOCK)
        o_s = jnp.einsum('bqhkj,bqkjd->bqhd', p_s.astype(vg.dtype), vg)
        return (o_c, o_s)

    return map_query_blocks(blk, (q,), pos_q)


def window_core(q, k, v, pq, pk):
    s = jnp.einsum('bnqhd,bnkd->bnhqk', q, k).astype(jnp.float32) * HEAD_DIM ** -0.5
    dist = pq[:, :, None] - pk[:, None, :]
    mask = (dist >= 0) & (dist <= NSA_WINDOW) & (pk[:, None, :] >= 0)
    p = masked_softmax(s, mask[None, :, None])
    return jnp.einsum('bnhqk,bnkd->bnqhd', p.astype(v.dtype), v)


def nsa_window(q, kw, vw, q_start, k_start):
    b, t, h, d = q.shape
    lk = kw.shape[1]
    if lk == t and t % QUERY_BLOCK == 0:
        nq = t // QUERY_BLOCK
        nw = NSA_WINDOW // QUERY_BLOCK
        bidx = jnp.arange(nq)[:, None] + jnp.arange(nw + 1)[None, :]
        band = lambda a: jnp.pad(a, ((0, 0), (NSA_WINDOW, 0), (0, 0))).reshape(
            b, nq + nw, QUERY_BLOCK, d)[:, bidx].reshape(b, nq, (nw + 1) * QUERY_BLOCK, d)
        pk = ((k_start - NSA_WINDOW + bidx * QUERY_BLOCK)[..., None]
              + jnp.arange(QUERY_BLOCK)).reshape(nq, (nw + 1) * QUERY_BLOCK)
        pq = (q_start + jnp.arange(t)).reshape(nq, QUERY_BLOCK)
        o = window_core(q.reshape(b, nq, QUERY_BLOCK, h, d), band(kw), band(vw), pq, pk)
        return o.reshape(b, t, h, d)
    pq = (q_start + jnp.arange(t))[None]
    pk = (k_start + jnp.arange(lk))[None]
    return window_core(q[:, None], kw[:, None], vw[:, None], pq, pk)[:, 0]


def decoder_layer(x, pe_in, fox_kv_past, fox_logf_past, nsa_kv_past, nsa_win_past, diff_kv_past,
                  w, layer_idx):
    (ln1_g, w_in, b_fox_f, b_nsa_gate, nsa_pe, nsa_phi, diff_lambda, diff_subln_g,
     w_up_fox, w_up_nsa, w_up_diff, w_merge_gate, b_merge_gate, w_o,
     ln2_g, w_ff1, w_ff2, w_ple, ln_ple_g, w_ple_gate) = w
    b, t, _ = x.shape
    past = fox_kv_past.shape[1]
    n_win = nsa_win_past.shape[1]
    total = past + t
    pos_q = past + jnp.arange(t, dtype=jnp.int32)
    pos_k = jnp.arange(total, dtype=jnp.int32)

    h = rmsnorm(x, ln1_g)
    (fq, fk, fv, ff, nq, nkc, nvc, nks, nvs, nkw, nvw, ng, dq, dk, dv) = jnp.split(
        h @ w_in, np.cumsum(IN_SIZES)[:-1].tolist(), axis=-1)

    fq = fq.reshape(b, t, FOX_HEADS, HEAD_DIM)
    fox_new = jnp.stack([fk.reshape(b, t, FOX_HEADS, HEAD_DIM),
                         fv.reshape(b, t, FOX_HEADS, HEAD_DIM)], axis=2)
    logf = jax.nn.log_sigmoid(ff.astype(jnp.float32) + b_fox_f.astype(jnp.float32))
    fox_all = jnp.concatenate([fox_kv_past, fox_new], axis=1)
    logf_all = jnp.concatenate([fox_logf_past.astype(jnp.float32), logf], axis=1)
    o_fox = fox_attention(fq, fox_all[:, :, 0], fox_all[:, :, 1], logf_all, pos_q, pos_k)

    rope1 = lambda r: rope(r[:, :, None], pos_q)[:, :, 0]
    nq = rope(nq.reshape(b, t, NSA_HEADS, HEAD_DIM), pos_q)
    nsa_new = jnp.stack([rope1(nkc), nvc, rope1(nks), nvs], axis=2)
    nsa_all = jnp.concatenate([nsa_kv_past, nsa_new], axis=1)
    o_cmp, o_sel = nsa_cmp_sel(nq, nsa_all[:, :, 0], nsa_all[:, :, 1], nsa_all[:, :, 2],
                               nsa_all[:, :, 3], nsa_pe, nsa_phi, pos_q)
    win_all = jnp.concatenate([nsa_win_past, jnp.stack([rope1(nkw), nvw], axis=2)], axis=1)
    o_win = nsa_window(nq, win_all[:, :, 0], win_all[:, :, 1], past, past - n_win)
    g = jax.nn.sigmoid(ng + b_nsa_gate).reshape(b, t, NSA_HEADS, 3)
    o_nsa = g[..., 0:1] * o_cmp + g[..., 1:2] * o_sel + g[..., 2:3] * o_win

    dq = rope(dq.reshape(b, t, 2 * DIFF_HEADS, HEAD_DIM), pos_q).reshape(b, t, DIFF_HEADS, 2 * HEAD_DIM)
    dk = rope(dk.reshape(b, t, 2 * DIFF_HEADS, HEAD_DIM), pos_q).reshape(b, t, DIFF_HEADS, 2 * HEAD_DIM)
    diff_new = jnp.stack([dk, dv.reshape(b, t, DIFF_HEADS, 2 * HEAD_DIM)], axis=2)
    diff_all = jnp.concatenate([diff_kv_past, diff_new], axis=1)
    lam_init = 0.8 - 0.6 * math.exp(-0.3 * layer_idx)
    dl = diff_lambda.astype(jnp.float32)
    lam = jnp.exp(jnp.sum(dl[0] * dl[1])) - jnp.exp(jnp.sum(dl[2] * dl[3])) + lam_init
    o_diff = diff_attention(dq, diff_all[:, :, 0], diff_all[:, :, 1], lam, pos_q, pos_k)
    o_diff = rmsnorm(o_diff, diff_subln_g) * (1.0 - lam_init)

    g_fox, g_nsa, g_diff = jnp.split(jax.nn.sigmoid(h @ w_merge_gate + b_merge_gate), 3, axis=-1)
    merged = (g_fox * (o_fox.reshape(b, t, FOX_W) @ w_up_fox)
              + g_nsa * (o_nsa.reshape(b, t, NSA_W) @ w_up_nsa)
              + g_diff * (o_diff.reshape(b, t, DIFF_W) @ w_up_diff))
    x = x + merged @ w_o

    x = x + jnp.square(jax.nn.relu(rmsnorm(x, ln2_g) @ w_ff1)) @ w_ff2

    x = x + jax.nn.sigmoid(rmsnorm(x, ln_ple_g) @ w_ple_gate) * (pe_in @ w_ple)

    new_win = win_all[:, win_all.shape[1] - min(NSA_WINDOW, total):]
    return x, fox_new, logf, nsa_new, new_win, diff_new


def setup_inputs(seed: int = 0) -> dict:
    key = jax.random.key(seed)
    ks = iter(jax.random.split(key, 40))
    nrm = lambda shape, s=1.0: s * jax.random.normal(next(ks), shape, jnp.float32)
    n_pages = PAST_LEN // PAGE_SIZE
    n_used = DEC_BATCH * n_pages
    n_pool = n_used + max(1, n_used // 4)
    win_keep = min(NSA_WINDOW, PAST_LEN)
    d = D_MODEL
    out = {}
    out['x_prompt'] = nrm((BATCH, SEQ, d))
    out['x_sample'] = nrm((DEC_BATCH, DEC_SEQ, d))
    out['cache_fox_kv'] = nrm((DEPTH, n_pool, PAGE_SIZE, 2, FOX_HEADS, HEAD_DIM))
    out['cache_fox_logf'] = jax.nn.log_sigmoid(FORGET_BIAS + nrm((DEPTH, n_pool, PAGE_SIZE, FOX_HEADS)))
    out['cache_nsa_kv'] = nrm((DEPTH, n_pool, PAGE_SIZE, 4, HEAD_DIM))
    out['state_nsa_win'] = nrm((DEPTH, DEC_BATCH, win_keep, 2, HEAD_DIM))
    out['cache_diff_kv'] = nrm((DEPTH, n_pool, PAGE_SIZE, 2, DIFF_HEADS, 2 * HEAD_DIM))
    out['page_table'] = jax.random.permutation(next(ks), n_pool)[:n_used].reshape(
        DEC_BATCH, n_pages).astype(jnp.int32)
    out['p_prompt'] = nrm((DEPTH, BATCH, SEQ, PLE_DIM))
    out['p_sample'] = nrm((DEPTH, DEC_BATCH, DEC_SEQ, PLE_DIM))
    out['ln1_g'] = 1.0 + nrm((DEPTH, d), 0.02)
    out['w_in'] = nrm((DEPTH, d, D_IN), d ** -0.5)
    out['b_fox_f'] = FORGET_BIAS + nrm((DEPTH, FOX_HEADS), 0.1)
    out['b_nsa_gate'] = nrm((DEPTH, 3 * NSA_HEADS), 0.02)
    out['nsa_pe'] = nrm((DEPTH, 2, NSA_BLOCK, HEAD_DIM), 0.1)
    out['nsa_phi'] = nrm((DEPTH, 2, HEAD_DIM, HEAD_DIM), HEAD_DIM ** -0.5)
    out['diff_lambda'] = nrm((DEPTH, 4, HEAD_DIM), 0.1)
    out['diff_subln_g'] = 1.0 + nrm((DEPTH, 2 * HEAD_DIM), 0.02)
    out['w_up_fox'] = nrm((DEPTH, FOX_W, d), FOX_W ** -0.5)
    out['w_up_nsa'] = nrm((DEPTH, NSA_W, d), NSA_W ** -0.5)
    out['w_up_diff'] = nrm((DEPTH, DIFF_W, d), DIFF_W ** -0.5)
    out['w_merge_gate'] = nrm((DEPTH, d, 3 * d), d ** -0.5)
    out['b_merge_gate'] = nrm((DEPTH, 3 * d), 0.02)
    out['w_o'] = nrm((DEPTH, d, d), d ** -0.5)
    out['ln2_g'] = 1.0 + nrm((DEPTH, d), 0.02)
    out['w_ff1'] = nrm((DEPTH, d, D_FF), d ** -0.5)
    out['w_ff2'] = nrm((DEPTH, D_FF, d), D_FF ** -0.5)
    out['w_ple'] = nrm((DEPTH, PLE_DIM, d), PLE_DIM ** -0.5)
    out['ln_ple_g'] = 1.0 + nrm((DEPTH, d), 0.02)
    out['w_ple_gate'] = nrm((DEPTH, d, d), d ** -0.5)
    out['final_norm_g'] = 1.0 + nrm((d,), 0.02)
    return out


def reference(x_prompt, x_sample, cache_fox_kv, cache_fox_logf, cache_nsa_kv, state_nsa_win,
              cache_diff_kv, page_table, p_prompt, p_sample,
              ln1_g, w_in, b_fox_f, b_nsa_gate, nsa_pe, nsa_phi, diff_lambda, diff_subln_g,
              w_up_fox, w_up_nsa, w_up_diff, w_merge_gate, b_merge_gate, w_o,
              ln2_g, w_ff1, w_ff2, w_ple, ln_ple_g, w_ple_gate, final_norm_g):
    xp, xs = x_prompt, x_sample
    bp = xp.shape[0]
    fkv_p, flf_p, nkv_p, nwin_p, dkv_p = [], [], [], [], []
    fkv_s, flf_s, nkv_s, nwin_s, dkv_s = [], [], [], [], []
    layer_params = (ln1_g, w_in, b_fox_f, b_nsa_gate, nsa_pe, nsa_phi, diff_lambda, diff_subln_g,
                    w_up_fox, w_up_nsa, w_up_diff, w_merge_gate, b_merge_gate, w_o,
                    ln2_g, w_ff1, w_ff2, w_ple, ln_ple_g, w_ple_gate)
    for i in range(DEPTH):
        w = tuple(a[i] for a in layer_params)
        empty = lambda c: jnp.zeros((bp, 0) + c.shape[3:], xp.dtype)
        xp, a0, a1, a2, a3, a4 = decoder_layer(
            xp, p_prompt[i], empty(cache_fox_kv), empty(cache_fox_logf), empty(cache_nsa_kv),
            empty(state_nsa_win), empty(cache_diff_kv), w, i)
        fkv_p.append(a0); flf_p.append(a1); nkv_p.append(a2); nwin_p.append(a3); dkv_p.append(a4)
        xs, s0, s1, s2, s3, s4 = decoder_layer(
            xs, p_sample[i], gather_pages(cache_fox_kv[i], page_table),
            gather_pages(cache_fox_logf[i], page_table), gather_pages(cache_nsa_kv[i], page_table),
            state_nsa_win[i], gather_pages(cache_diff_kv[i], page_table), w, i)
        fkv_s.append(s0); flf_s.append(s1); nkv_s.append(s2); nwin_s.append(s3); dkv_s.append(s4)
    y_prompt = rmsnorm(xp, final_norm_g)
    y_sample = rmsnorm(xs, final_norm_g)
    new_fox_kv_p = jnp.stack(fkv_p)
    new_fox_logf_p = jnp.stack(flf_p)
    new_nsa_kv_p = jnp.stack(nkv_p)
    new_nsa_win_p = jnp.stack(nwin_p)
    new_diff_kv_p = jnp.stack(dkv_p)
    new_fox_kv_s = jnp.stack(fkv_s)
    new_fox_logf_s = jnp.stack(flf_s)
    new_nsa_kv_s = jnp.stack(nkv_s)
    new_nsa_win_s = jnp.stack(nwin_s)
    new_diff_kv_s = jnp.stack(dkv_s)
    return (y_prompt, y_sample, new_fox_kv_p, new_fox_logf_p, new_nsa_kv_p, new_nsa_win_p,
            new_diff_kv_p, new_fox_kv_s, new_fox_logf_s, new_nsa_kv_s, new_nsa_win_s, new_diff_kv_s)
```

```python
import functools
import math

import numpy as np
import jax
import jax.numpy as jnp
from jax import lax
from jax.experimental import pallas as pl
from jax.experimental.pallas import tpu as pltpu

F32 = jnp.float32
BF16 = jnp.bfloat16

D_MODEL = 1024
HEAD_DIM = 64
FOX_HEADS = 4
NSA_HEADS = 4
NSA_BLOCK = 64
NSA_TOPK = 16
NSA_WINDOW = 512
DIFF_HEADS = 4
FOX_W = FOX_HEADS * HEAD_DIM
NSA_W = NSA_HEADS * HEAD_DIM
DIFF_W = DIFF_HEADS * 2 * HEAD_DIM
D_FF = 4 * D_MODEL
ROPE_THETA = 10000.0
RMS_EPS = 1e-6
NEG_INF = -1e30
PAGE_SIZE = 128
IN_SIZES = (FOX_W, FOX_W, FOX_W, FOX_HEADS,
            NSA_W, HEAD_DIM, HEAD_DIM, HEAD_DIM, HEAD_DIM, HEAD_DIM, HEAD_DIM, 3 * NSA_HEADS,
            DIFF_W, DIFF_W, DIFF_W)
SCALE = HEAD_DIM ** -0.5
LANES = 128
SUBLANES = 8
VMEM_LIMIT = 56 * 1024 * 1024

C_FQ, C_FKV, C_NQ, C_NSA, C_WIN, C_DQ, C_DK, C_DV, C_SMALL, C_END = (
    0, 256, 768, 1024, 1280, 1408, 1920, 2432, 2944, 3072)


def _dot(a, b):
    return jnp.dot(a, b, preferred_element_type=F32)


def _dot_nt(a, b):
    return lax.dot_general(a, b, (((1,), (1,)), ((), ())), preferred_element_type=F32)


def _bf(x):
    return x.astype(BF16)


def _dot_split3(a, b):
    hi = _bf(a)
    r1 = a - hi.astype(F32)
    mid = _bf(r1)
    lo = _bf(r1 - mid.astype(F32))
    return _dot(hi, b) + _dot(mid, b) + _dot(lo, b)


def _rms(x, g):
    return x * lax.rsqrt(jnp.mean(x * x, axis=-1, keepdims=True) + RMS_EPS) * g


def _params(sem):
    return pltpu.CompilerParams(dimension_semantics=sem, vmem_limit_bytes=VMEM_LIMIT)


def _full(a):
    nd = a.ndim
    return pl.BlockSpec(a.shape, lambda *_: (0,) * nd)


def _rope(y, cos, sin):
    lane = lax.broadcasted_iota(jnp.int32, (1, LANES), 1)
    first_half = (lane % HEAD_DIM) < (HEAD_DIM // 2)
    outs = []
    for c in range(y.shape[1] // LANES):
        blk = y[:, c * LANES:(c + 1) * LANES]
        partner = jnp.where(first_half,
                            pltpu.roll(blk, LANES - HEAD_DIM // 2, 1),
                            pltpu.roll(blk, HEAD_DIM // 2, 1))
        outs.append(blk * cos + partner * sin)
    return outs[0] if len(outs) == 1 else jnp.concatenate(outs, axis=1)


def _in_proj_kernel(x_ref, g_ref, w_ref, b_ref, tab_ref,
                    fq_ref, fkv_ref, fkvb_ref, nq_ref, nsa_ref, nsab_ref, win_ref, winb_ref,
                    dq_ref, dkv_ref, dkvb_ref, small_ref):
    h = _bf(_rms(x_ref[...], g_ref[...]))
    seg = lambda a, b: _dot(h, w_ref[:, a:b])
    cos2, sin2 = tab_ref[:, 0:128], tab_ref[:, 128:256]
    cos1, sin1 = tab_ref[:, 256:384], tab_ref[:, 384:512]

    fq_ref[...] = _bf(seg(C_FQ, C_FKV) * SCALE)
    y = seg(C_FKV, C_NQ)
    fkv_ref[...] = y
    fkvb_ref[...] = _bf(y)
    nq_ref[...] = _bf(_rope(seg(C_NQ, C_NSA), cos2, sin2) * SCALE)
    y = _rope(seg(C_NSA, C_WIN), cos1, sin1)
    nsa_ref[...] = y
    nsab_ref[...] = _bf(y)
    y = _rope(seg(C_WIN, C_DQ), cos1, sin1)
    win_ref[...] = y
    winb_ref[...] = _bf(y)
    dq_ref[...] = _bf(_rope(seg(C_DQ, C_DK), cos2, sin2) * SCALE)
    y = _rope(seg(C_DK, C_DV), cos2, sin2)
    dkv_ref[:, 0:DIFF_W] = y
    dkvb_ref[:, 0:DIFF_W] = _bf(y)
    y = seg(C_DV, C_SMALL)
    dkv_ref[:, DIFF_W:2 * DIFF_W] = y
    dkvb_ref[:, DIFF_W:2 * DIFF_W] = _bf(y)
    z = seg(C_SMALL, C_END) + b_ref[...]
    lane = lax.broadcasted_iota(jnp.int32, (1, LANES), 1)
    log_sig = jnp.minimum(z, 0.0) - jnp.log1p(jnp.exp(-jnp.abs(z)))
    small_ref[...] = jnp.where(lane < FOX_HEADS, log_sig, jax.nn.sigmoid(z))


def _in_proj(x, ln_g, w_re, b_small, tab, tm):
    n = x.shape[0]
    period = tab.shape[0] // tm
    row = lambda w: pl.BlockSpec((tm, w), lambda i: (i, 0))
    widths = (FOX_W, 2 * FOX_W, 2 * FOX_W, NSA_W, 256, 256, 128, 128, DIFF_W, 2 * DIFF_W, 2 * DIFF_W, LANES)
    dtypes = (BF16, F32, BF16, BF16, F32, BF16, F32, BF16, BF16, F32, BF16, F32)
    return pl.pallas_call(
        _in_proj_kernel,
        grid=(n // tm,),
        in_specs=[row(D_MODEL), _full(ln_g), _full(w_re), _full(b_small),
                  pl.BlockSpec((tm, 512), lambda i: (i % period, 0))],
        out_specs=[row(w) for w in widths],
        out_shape=[jax.ShapeDtypeStruct((n, w), d) for w, d in zip(widths, dtypes)],
        compiler_params=_params(("arbitrary",)),
        name="in_proj",
    )(x, ln_g, w_re, b_small, tab)


def _cumsum_kernel(x_ref, o_ref):
    x = x_ref[0]
    t = x.shape[1]
    idx = lax.broadcasted_iota(jnp.int32, x.shape, 1)
    shift = 1
    while shift < t:
        x = x + jnp.where(idx >= shift, pltpu.roll(x, shift, 1), 0.0)
        shift *= 2
    o_ref[0] = x


def _cumsum_lanes(x):
    b, h, t = x.shape
    x = jnp.pad(x, ((0, 0), (0, SUBLANES - h), (0, 0)))
    spec = pl.BlockSpec((1, SUBLANES, t), lambda i: (i, 0, 0))
    return pl.pallas_call(
        _cumsum_kernel, grid=(b,), in_specs=[spec], out_specs=spec,
        out_shape=jax.ShapeDtypeStruct(x.shape, F32),
        compiler_params=_params(("arbitrary",)), name="fox_cumsum")(x)[:, :h]


def _flash_update(s, v_b, m, l, acc):
    m_new = jnp.maximum(m, jnp.max(s, axis=-1, keepdims=True))
    alpha = jnp.exp(m - m_new)
    p = jnp.exp(s - m_new)
    l = alpha * l + jnp.sum(p, axis=-1, keepdims=True)
    acc = alpha * acc + _dot(_bf(p), v_b)
    return m_new, l, acc


def _flash_init(rows, dv):
    return (jnp.full((rows, 1), NEG_INF, F32), jnp.zeros((rows, 1), F32), jnp.zeros((rows, dv), F32))


def _causal_tile(t):
    return (lax.broadcasted_iota(jnp.int32, (t, t), 0) >= lax.broadcasted_iota(jnp.int32, (t, t), 1))


def _fox_kernel(q_ref, kv_ref, ck_ref, cq_ref, o_ref, *, tq):
    i = pl.program_id(1)
    causal = _causal_tile(tq)
    for h in range(FOX_HEADS):
        q = q_ref[0, :, h * HEAD_DIM:(h + 1) * HEAD_DIM]
        cq = cq_ref[0, :, h:h + 1]

        def step(j, carry, q=q, cq=cq, h=h, diag=False):
            rows = pl.ds(pl.multiple_of(j * tq, tq), tq)
            k = kv_ref[0, rows, h * HEAD_DIM:(h + 1) * HEAD_DIM]
            v = kv_ref[0, rows, FOX_W + h * HEAD_DIM:FOX_W + (h + 1) * HEAD_DIM]
            s = _dot_nt(q, k) + (cq - ck_ref[0, j, h:h + 1, :])
            if diag:
                s = jnp.where(causal, s, NEG_INF)
            return _flash_update(s, v, *carry)

        carry = lax.fori_loop(0, i, step, _flash_init(tq, HEAD_DIM))
        _, l, acc = step(i, carry, diag=True)
        o_ref[0, :, h * HEAD_DIM:(h + 1) * HEAD_DIM] = acc / l


def _fox_prompt(q_b, kv_b, ck, cq, tq):
    b, t, _ = q_b.shape
    return pl.pallas_call(
        functools.partial(_fox_kernel, tq=tq),
        grid=(b, t // tq),
        in_specs=[pl.BlockSpec((1, tq, FOX_W), lambda bi, i: (bi, i, 0)),
                  pl.BlockSpec((1, t, 2 * FOX_W), lambda bi, i: (bi, 0, 0)),
                  pl.BlockSpec((1, t // tq, FOX_HEADS, tq), lambda bi, i: (bi, 0, 0, 0)),
                  pl.BlockSpec((1, tq, FOX_HEADS), lambda bi, i: (bi, i, 0))],
        out_specs=pl.BlockSpec((1, tq, FOX_W), lambda bi, i: (bi, i, 0)),
        out_shape=jax.ShapeDtypeStruct((b, t, FOX_W), F32),
        compiler_params=_params(("arbitrary", "arbitrary")),
        name="fox_prompt",
    )(q_b, kv_b, ck, cq)


def _diff_lambda(dl_ref, li_ref):
    dl = dl_ref[...]
    a = jnp.sum(dl[0:1] * dl[1:2], axis=-1, keepdims=True)
    b = jnp.sum(dl[2:3] * dl[3:4], axis=-1, keepdims=True)
    return jnp.exp(a) - jnp.exp(b) + li_ref[0:1, 0:1]


def _diff_kernel(q_ref, kv_ref, dl_ref, li_ref, o_ref, *, tq):
    i = pl.program_id(1)
    causal = _causal_tile(tq)
    lam = _diff_lambda(dl_ref, li_ref)
    hw = 2 * HEAD_DIM
    for h in range(DIFF_HEADS):
        q1 = q_ref[0, :, h * hw:h * hw + HEAD_DIM]
        q2 = q_ref[0, :, h * hw + HEAD_DIM:(h + 1) * hw]

        def step(j, carry, q1=q1, q2=q2, h=h, diag=False):
            rows = pl.ds(pl.multiple_of(j * tq, tq), tq)
            k1 = kv_ref[0, rows, h * hw:h * hw + HEAD_DIM]
            k2 = kv_ref[0, rows, h * hw + HEAD_DIM:(h + 1) * hw]
            v = kv_ref[0, rows, DIFF_W + h * hw:DIFF_W + (h + 1) * hw]
            s1 = _dot_nt(q1, k1)
            s2 = _dot_nt(q2, k2)
            if diag:
                s1 = jnp.where(causal, s1, NEG_INF)
                s2 = jnp.where(causal, s2, NEG_INF)
            return _flash_update(s1, v, *carry[:3]) + _flash_update(s2, v, *carry[3:])

        carry = lax.fori_loop(0, i, step, _flash_init(tq, hw) + _flash_init(tq, hw))
        _, l1, a1, _, l2, a2 = step(i, carry, diag=True)
        o_ref[0, :, h * hw:(h + 1) * hw] = a1 / l1 - lam * (a2 / l2)


def _diff_prompt(q_b, kv_b, dl, li, tq):
    b, t, _ = q_b.shape
    return pl.pallas_call(
        functools.partial(_diff_kernel, tq=tq),
        grid=(b, t // tq),
        in_specs=[pl.BlockSpec((1, tq, DIFF_W), lambda bi, i: (bi, i, 0)),
                  pl.BlockSpec((1, t, 2 * DIFF_W), lambda bi, i: (bi, 0, 0)),
                  _full(dl), _full(li)],
        out_specs=pl.BlockSpec((1, tq, DIFF_W), lambda bi, i: (bi, i, 0)),
        out_shape=jax.ShapeDtypeStruct((b, t, DIFF_W), F32),
        compiler_params=_params(("arbitrary", "arbitrary")),
        name="diff_prompt",
    )(q_b, kv_b, dl, li)


def _compress_kernel(x_ref, pe_ref, phi_ref, o_ref):
    x = x_ref[0]
    nb = x.shape[0] // NSA_BLOCK
    mean = jnp.sum(x.reshape(nb, NSA_BLOCK, LANES), axis=1) * (1.0 / NSA_BLOCK)
    pe_mean = jnp.sum(pe_ref[0], axis=0, keepdims=True) * (1.0 / NSA_BLOCK)
    o_ref[0] = _dot(_bf(mean + pe_mean), phi_ref[0])


def _row_block(rows, cap=4096):
    rb = cap
    while rows % rb:
        rb //= 2
    assert rb >= NSA_BLOCK
    return rb


def _compress(x, pe_cat, phi_bd, rb):
    nl, r, _ = x.shape
    return pl.pallas_call(
        _compress_kernel,
        grid=(nl, r // rb),
        in_specs=[pl.BlockSpec((1, rb, LANES), lambda l, i: (l, i, 0)),
                  pl.BlockSpec((1, NSA_BLOCK, LANES), lambda l, i: (l, 0, 0)),
                  pl.BlockSpec((1, LANES, LANES), lambda l, i: (l, 0, 0))],
        out_specs=pl.BlockSpec((1, rb // NSA_BLOCK, LANES), lambda l, i: (l, i, 0)),
        out_shape=jax.ShapeDtypeStruct((nl, r // NSA_BLOCK, LANES), F32),
        compiler_params=_params(("arbitrary", "arbitrary")),
        name="nsa_compress",
    )(x, pe_cat, phi_bd)


def _topk_membership(score, n_valid, n_sel):
    blk = lax.broadcasted_iota(jnp.int32, score.shape, 1)
    rank = jnp.zeros(score.shape, F32)
    for n in range(n_valid):
        col = score[:, n:n + 1]
        beats = (col > score) | ((col == score) & (blk > n))
        rank = rank + beats.astype(F32)
    return ((rank < n_sel) & (blk < n_valid)).astype(F32)


def _stack_heads(x, n_heads):
    return jnp.concatenate([x[:, h * HEAD_DIM:(h + 1) * HEAD_DIM] for h in range(n_heads)], axis=0)


def _unstack_heads(x, n_heads):
    rows = x.shape[0] // n_heads
    return jnp.concatenate([x[h * rows:(h + 1) * rows] for h in range(n_heads)], axis=1)


def _nsa_kernel(q_ref, cb_ref, kv_ref, win_ref, g_ref, o_ref, *, tq, n_sel):
    i = pl.program_id(1)
    nb = cb_ref.shape[1]
    q = q_ref[0]
    q4 = _stack_heads(q, NSA_HEADS)
    n4 = NSA_HEADS * tq
    row = lax.broadcasted_iota(jnp.int32, (tq, 1), 0)
    row4 = lax.broadcasted_iota(jnp.int32, (n4, 1), 0) % tq
    pq = i * tq + row
    pq4 = i * tq + row4
    blk = lax.broadcasted_iota(jnp.int32, (1, nb), 1)

    cb = cb_ref[0]
    kcb, vcb = _bf(cb[:, 0:HEAD_DIM]), _bf(cb[:, HEAD_DIM:])
    done4 = ((blk + 1) * NSA_BLOCK - 1) <= pq4
    s = jnp.where(done4, _dot_nt(q4, kcb), NEG_INF)
    e = jnp.exp(s - jnp.max(s, axis=-1, keepdims=True))
    p_c = e / jnp.sum(e, axis=-1, keepdims=True) * done4.astype(F32)
    o_cmp = _dot(_bf(p_c), vcb)
    imp = p_c[0:tq]
    for h in range(1, NSA_HEADS):
        imp = imp + p_c[h * tq:(h + 1) * tq]

    cur = pq // NSA_BLOCK
    forced = (blk == 0) | (blk == cur) | (blk == cur - 1)
    score = jnp.where(forced, jnp.inf, jnp.where(blk <= cur, imp, -jnp.inf))
    sel = _bf(_topk_membership(score, nb, n_sel))
    sel4 = jnp.concatenate([sel] * NSA_HEADS, axis=0)

    col = lax.broadcasted_iota(jnp.int32, (1, tq), 1)
    blk_col = lax.broadcasted_iota(jnp.int32, (nb, 1), 0)

    def sel_step(j, carry, diag=False):
        rows = pl.ds(pl.multiple_of(j * tq, tq), tq)
        k = kv_ref[0, rows, 0:HEAD_DIM]
        v = kv_ref[0, rows, HEAD_DIM:2 * HEAD_DIM]
        expand = _bf((blk_col == (j * tq + col) // NSA_BLOCK).astype(F32))
        mask = _dot(sel4, expand) > 0.5
        if diag:
            mask = mask & (row4 >= col)
        return _flash_update(jnp.where(mask, _dot_nt(q4, k), NEG_INF), v, *carry)

    carry = sel_step(i, _flash_init(n4, HEAD_DIM), diag=True)
    _, l, acc = lax.fori_loop(0, i, sel_step, carry)
    o_sel = acc / l

    carry = _flash_init(n4, HEAD_DIM)
    for d in range(NSA_WINDOW // tq + 1):
        j = jnp.maximum(i - d, 0)
        rows = pl.ds(pl.multiple_of(j * tq, tq), tq)
        k = win_ref[0, rows, 0:HEAD_DIM]
        v = win_ref[0, rows, HEAD_DIM:2 * HEAD_DIM]
        dist = d * tq + row4 - col + jnp.where(i - d >= 0, 0, 2 * NSA_WINDOW + tq)
        mask = (dist >= 0) & (dist <= NSA_WINDOW)
        carry = _flash_update(jnp.where(mask, _dot_nt(q4, k), NEG_INF), v, *carry)
    o_win = carry[2] / carry[1]

    g = g_ref[0]
    for h in range(NSA_HEADS):
        sl = slice(h * tq, (h + 1) * tq)
        c0 = FOX_HEADS + 3 * h
        o_ref[0, :, h * HEAD_DIM:(h + 1) * HEAD_DIM] = (
            g[:, c0:c0 + 1] * o_cmp[sl] + g[:, c0 + 1:c0 + 2] * o_sel[sl] + g[:, c0 + 2:c0 + 3] * o_win[sl])


def _nsa_prompt(q_b, cb, nsa_b, win_b, small, tq):
    b, t, _ = q_b.shape
    nb = t // NSA_BLOCK
    return pl.pallas_call(
        functools.partial(_nsa_kernel, tq=tq, n_sel=min(NSA_TOPK, nb)),
        grid=(b, t // tq),
        in_specs=[pl.BlockSpec((1, tq, NSA_W), lambda bi, i: (bi, i, 0)),
                  pl.BlockSpec((1, nb, LANES), lambda bi, i: (bi, 0, 0)),
                  pl.BlockSpec((1, t, LANES), lambda bi, i: (bi, 0, 1)),
                  pl.BlockSpec((1, t, LANES), lambda bi, i: (bi, 0, 0)),
                  pl.BlockSpec((1, tq, LANES), lambda bi, i: (bi, i, 0))],
        out_specs=pl.BlockSpec((1, tq, NSA_W), lambda bi, i: (bi, i, 0)),
        out_shape=jax.ShapeDtypeStruct((b, t, NSA_W), F32),
        compiler_params=_params(("arbitrary", "arbitrary")),
        name="nsa_prompt",
    )(q_b, cb, nsa_b, win_b, small)


DEC_ROWS = 16


def _head_rows(x, n_heads):
    n = x.shape[0]
    return jnp.pad(x.reshape(n, n_heads, HEAD_DIM), ((0, 0), (0, DEC_ROWS - n_heads), (0, 0)))


def _dec_select_kernel(pt_ref, q_ref, cbp_ref, ocmp_ref, sel_ref, cb_scr, *, n_pages, past, n_sel):
    b = pl.program_id(0)
    n_cmp = (past + 1) // NSA_BLOCK
    n_blk = -(-(past + 1) // NSA_BLOCK)
    per_page = PAGE_SIZE // NSA_BLOCK
    cb_scr[...] = jnp.zeros(cb_scr.shape, F32)
    for p in range(n_pages):
        cb_scr[p * per_page:(p + 1) * per_page, :] = cbp_ref[pt_ref[b, p]]
    cb = cb_scr[...]
    kcb, vcb = _bf(cb[:, 0:HEAD_DIM]), _bf(cb[:, HEAD_DIM:])
    blk = lax.broadcasted_iota(jnp.int32, (1, LANES), 1)
    done = (blk < n_cmp) & (((blk + 1) * NSA_BLOCK - 1) <= past)
    s = jnp.where(done, _dot_nt(q_ref[0], kcb), NEG_INF)
    e = jnp.exp(s - jnp.max(s, axis=-1, keepdims=True))
    p_c = e / jnp.sum(e, axis=-1, keepdims=True) * done.astype(F32)
    ocmp_ref[0] = _dot(_bf(p_c), vcb)
    head_row = lax.broadcasted_iota(jnp.int32, (DEC_ROWS, 1), 0) < NSA_HEADS
    imp = jnp.sum(jnp.where(head_row, p_c, 0.0), axis=0, keepdims=True)
    imp = jnp.where(blk < n_cmp, imp, -jnp.inf)
    cur = past // NSA_BLOCK
    forced = (blk == 0) | (blk == cur) | (blk == cur - 1)
    score = jnp.where(forced, jnp.inf, jnp.where(blk <= cur, imp, -jnp.inf))
    sel_ref[0] = _topk_membership(score, n_blk, n_sel).astype(jnp.int32)


def _dec_select(page_table, nq16, cb_pool, past):
    nb, n_pages = page_table.shape
    n_blk = -(-(past + 1) // NSA_BLOCK)
    assert n_pages * (PAGE_SIZE // NSA_BLOCK) <= LANES and n_blk <= LANES
    kern = functools.partial(_dec_select_kernel, n_pages=n_pages, past=past, n_sel=min(NSA_TOPK, n_blk))
    return pl.pallas_call(
        kern,
        grid_spec=pltpu.PrefetchScalarGridSpec(
            num_scalar_prefetch=1, grid=(nb,),
            in_specs=[pl.BlockSpec((1, DEC_ROWS, HEAD_DIM), lambda b, pt: (b, 0, 0)),
                      pl.BlockSpec(cb_pool.shape, lambda b, pt: (0, 0, 0))],
            out_specs=[pl.BlockSpec((1, DEC_ROWS, HEAD_DIM), lambda b, pt: (b, 0, 0)),
                       pl.BlockSpec((1, 1, LANES), lambda b, pt: (b, 0, 0))],
            scratch_shapes=[pltpu.VMEM((LANES, LANES), F32)]),
        out_shape=[jax.ShapeDtypeStruct((nb, DEC_ROWS, HEAD_DIM), F32),
                   jax.ShapeDtypeStruct((nb, 1, LANES), jnp.int32)],
        compiler_params=_params(("arbitrary",)),
        name="nsa_dec_select",
    )(page_table, nq16, cb_pool)


def _pick_lane_per_row(x_row, lane_of_row):
    lane = lax.broadcasted_iota(jnp.int32, (lane_of_row.shape[0], LANES), 1)
    return jnp.sum(jnp.where(lane == lane_of_row, x_row, 0.0), axis=1, keepdims=True)


def _masked_rows(x_row, n_rows, group):
    w = x_row.shape[1]
    lane = lax.broadcasted_iota(jnp.int32, (n_rows, w), 1)
    row = lax.broadcasted_iota(jnp.int32, (n_rows, w), 0)
    return _bf(jnp.where(lane // group == row, x_row.astype(F32), 0.0))


def _dec_sweep_kernel(pt_ref, sel_ref,
                      fkv_ref, lfT_ref, nsa_ref, dkv_ref, win_ref,
                      fq_ref, fnew_ref, small_ref, nq_ref, nnew_ref, wnew_ref, dq_ref, dnew_ref,
                      ocmp_ref, dl_ref, li_ref,
                      ofox_ref, onsa_ref, odiff_ref,
                      fm, fl, facc, fcarry, sm, sl, sacc, dm, dl_s, dacc,
                      *, n_pages, past):
    b = pl.program_id(0)
    j = pl.program_id(1)
    page = n_pages - 1 - j
    row8 = lax.broadcasted_iota(jnp.int32, (SUBLANES, 1), 0)
    row16 = lax.broadcasted_iota(jnp.int32, (DEC_ROWS, 1), 0)
    hw = 2 * HEAD_DIM

    fq8 = _masked_rows(fq_ref[0], SUBLANES, HEAD_DIM)
    nq8 = nq_ref[0]
    dq16 = _masked_rows(dq_ref[0], DEC_ROWS, HEAD_DIM)
    small = small_ref[0]

    @pl.when(j == 0)
    def _init():
        fnew = fnew_ref[0]
        knew, vnew = _bf(fnew[:, 0:FOX_W]).astype(F32), _bf(fnew[:, FOX_W:]).astype(F32)
        fm[...] = jnp.broadcast_to(jnp.sum(fq8.astype(F32) * knew, axis=1, keepdims=True), fm.shape)
        fl[...] = jnp.ones(fl.shape, F32)
        facc[...] = jnp.broadcast_to(vnew, facc.shape)
        fcarry[...] = jnp.broadcast_to(_pick_lane_per_row(small, row8), fcarry.shape)
        nnew = nnew_ref[0]
        ks_new = _bf(nnew[:, 2 * HEAD_DIM:3 * HEAD_DIM]).astype(F32)
        vs_new = _bf(nnew[:, 3 * HEAD_DIM:]).astype(F32)
        sm[...] = jnp.broadcast_to(jnp.sum(nq8.astype(F32) * ks_new, axis=1, keepdims=True), sm.shape)
        sl[...] = jnp.ones(sl.shape, F32)
        sacc[...] = jnp.broadcast_to(vs_new, sacc.shape)
        dnew = dnew_ref[0]
        dk_new, dv_new = _bf(dnew[:, 0:DIFF_W]).astype(F32), _bf(dnew[:, DIFF_W:]).astype(F32)
        dm[...] = jnp.broadcast_to(jnp.sum(dq16.astype(F32) * dk_new, axis=1, keepdims=True), dm.shape)
        dl_s[...] = jnp.ones(dl_s.shape, F32)
        dacc[...] = jnp.broadcast_to(dv_new, dacc.shape)

    def update(s, v_b, m_ref, l_ref, acc_ref):
        m_old = m_ref[:, 0:1]
        m_new = jnp.maximum(m_old, jnp.max(s, axis=-1, keepdims=True))
        alpha = jnp.exp(m_old - m_new)
        p = jnp.exp(s - m_new)
        l_ref[...] = jnp.broadcast_to(alpha * l_ref[:, 0:1] + jnp.sum(p, axis=-1, keepdims=True), l_ref.shape)
        acc_ref[...] = alpha * acc_ref[...] + _dot(_bf(p), v_b)
        m_ref[...] = jnp.broadcast_to(m_new, m_ref.shape)

    lane_k = lax.broadcasted_iota(jnp.int32, (LANES, LANES), 1)
    row_k = lax.broadcasted_iota(jnp.int32, (LANES, LANES), 0)
    after = _bf((row_k > lane_k).astype(F32))
    lf8 = lfT_ref[...]
    carry = fcarry[:, 0:1]
    bias = _dot_split3(lf8, after) + carry
    fkv = fkv_ref[...]
    s = _dot_nt(fq8, _bf(fkv[:, 0:FOX_W])) + bias
    update(s, _bf(fkv[:, FOX_W:]), fm, fl, facc)
    fcarry[...] = jnp.broadcast_to(carry + jnp.sum(lf8, axis=1, keepdims=True), fcarry.shape)

    lane = lax.broadcasted_iota(jnp.int32, (1, LANES), 1)
    per_page = PAGE_SIZE // NSA_BLOCK
    chosen = jnp.where(lane < NSA_BLOCK, sel_ref[b, page * per_page], sel_ref[b, page * per_page + 1]) > 0
    nsa = nsa_ref[...]
    s = jnp.where(chosen, _dot_nt(nq8, _bf(nsa[:, 0:HEAD_DIM])), NEG_INF)
    update(s, _bf(nsa[:, HEAD_DIM:]), sm, sl, sacc)

    dkv = dkv_ref[...]
    s = _dot_nt(dq16, _bf(dkv[:, 0:DIFF_W]))
    update(s, _bf(dkv[:, DIFF_W:]), dm, dl_s, dacc)

    @pl.when(j == n_pages - 1)
    def _finish():
        lane_f = lax.broadcasted_iota(jnp.int32, (SUBLANES, FOX_W), 1)
        keep = (lane_f // HEAD_DIM) == row8
        ofox_ref[0] = jnp.sum(jnp.where(keep, facc[...] / fl[:, 0:1], 0.0), axis=0, keepdims=True)

        win = win_ref[0]
        wnew = wnew_ref[0]
        n_win = win.shape[0]
        idx = lax.broadcasted_iota(jnp.int32, (1, n_win), 1)
        dist = past - (past - n_win + idx)
        ok = (dist >= 0) & (dist <= NSA_WINDOW) & ((past - n_win + idx) >= 0)
        s = jnp.where(ok, _dot_nt(nq8, _bf(win[:, 0:HEAD_DIM])), NEG_INF)
        s_new = jnp.sum(nq8.astype(F32) * _bf(wnew[:, 0:HEAD_DIM]).astype(F32), axis=1, keepdims=True)
        m = jnp.maximum(jnp.max(s, axis=-1, keepdims=True), s_new)
        p = jnp.exp(s - m)
        p_new = jnp.exp(s_new - m)
        l = jnp.sum(p, axis=-1, keepdims=True) + p_new
        o_win = (_dot(_bf(p), _bf(win[:, HEAD_DIM:]))
                 + _bf(p_new).astype(F32) * _bf(wnew[:, HEAD_DIM:]).astype(F32)) / l
        o_sel = sacc[...] / sl[:, 0:1]
        g = [_pick_lane_per_row(small, FOX_HEADS + 3 * row16 + c) for c in range(3)]
        onsa_ref[0] = g[0] * ocmp_ref[0] + g[1] * o_sel + g[2] * o_win

        lam = _diff_lambda(dl_ref, li_ref)
        coef = jnp.where(row16 % 2 == 0, 1.0, -lam) / dl_s[:, 0:1]
        lane_d = lax.broadcasted_iota(jnp.int32, (DEC_ROWS, DIFF_W), 1)
        keep = (lane_d // hw) == (row16 // 2)
        odiff_ref[0] = jnp.sum(jnp.where(keep, dacc[...] * coef, 0.0), axis=0, keepdims=True)


def _dec_sweep(layer, page_table, sel, caches, win_state, new, o_cmp, dl, li, past):
    nb, n_pages = page_table.shape
    fox_kv, fox_lfT, nsa_kv, diff_kv = caches
    fq_b, fkv, small, nq16, nsa_new, win_new, dq_b, dkv = new
    n_win = win_state.shape[2]
    pg = lambda b, j, pt, sl: pt[b, n_pages - 1 - j]
    per_seq = lambda w: pl.BlockSpec((1, 1, w), lambda b, j, pt, sl: (b, 0, 0))
    head_rows = pl.BlockSpec((1, DEC_ROWS, HEAD_DIM), lambda b, j, pt, sl: (b, 0, 0))
    r3 = lambda a: a.reshape(nb, 1, a.shape[-1])
    stat = lambda rows: pltpu.VMEM((rows, LANES), F32)
    kern = functools.partial(_dec_sweep_kernel, n_pages=n_pages, past=past)
    return pl.pallas_call(
        kern,
        grid_spec=pltpu.PrefetchScalarGridSpec(
            num_scalar_prefetch=2, grid=(nb, n_pages),
            in_specs=[
                pl.BlockSpec((None, None, PAGE_SIZE, 2 * FOX_W), lambda b, j, pt, sl: (layer, pg(b, j, pt, sl), 0, 0)),
                pl.BlockSpec((None, None, SUBLANES, PAGE_SIZE), lambda b, j, pt, sl: (layer, pg(b, j, pt, sl), 0, 0)),
                pl.BlockSpec((None, None, PAGE_SIZE, LANES), lambda b, j, pt, sl: (layer, pg(b, j, pt, sl), 0, 1)),
                pl.BlockSpec((None, None, PAGE_SIZE, 2 * DIFF_W), lambda b, j, pt, sl: (layer, pg(b, j, pt, sl), 0, 0)),
                pl.BlockSpec((None, 1, n_win, LANES), lambda b, j, pt, sl: (layer, b, 0, 0)),
                per_seq(FOX_W), per_seq(2 * FOX_W), per_seq(LANES), head_rows, per_seq(256),
                per_seq(LANES), per_seq(DIFF_W), per_seq(2 * DIFF_W), head_rows,
                pl.BlockSpec(dl.shape, lambda b, j, pt, sl: (0, 0)),
                pl.BlockSpec(li.shape, lambda b, j, pt, sl: (0, 0)),
            ],
            out_specs=[per_seq(FOX_W), head_rows, per_seq(DIFF_W)],
            scratch_shapes=[stat(SUBLANES), stat(SUBLANES), pltpu.VMEM((SUBLANES, FOX_W), F32), stat(SUBLANES),
                            stat(DEC_ROWS), stat(DEC_ROWS), pltpu.VMEM((DEC_ROWS, HEAD_DIM), F32),
                            stat(DEC_ROWS), stat(DEC_ROWS), pltpu.VMEM((DEC_ROWS, DIFF_W), F32)]),
        out_shape=[jax.ShapeDtypeStruct((nb, 1, FOX_W), F32),
                   jax.ShapeDtypeStruct((nb, DEC_ROWS, HEAD_DIM), F32),
                   jax.ShapeDtypeStruct((nb, 1, DIFF_W), F32)],
        compiler_params=_params(("arbitrary", "arbitrary")),
        name="dec_sweep",
    )(page_table, sel, fox_kv, fox_lfT, nsa_kv, diff_kv, win_state,
      r3(fq_b), r3(fkv), r3(small), nq16, r3(nsa_new), r3(win_new), r3(dq_b), r3(dkv),
      o_cmp, dl, li)


def _post_kernel(x_ref, g1_ref, of_ref, on_ref, od_ref, gs_ref, wuf_ref, wun_ref, wud_ref,
                 wg_ref, bg_ref, wo_ref, o_ref):
    x = x_ref[...]
    h = _bf(_rms(x, g1_ref[...]))
    hw = 2 * HEAD_DIM
    od = od_ref[...]
    od = jnp.concatenate([_rms(od[:, k * hw:(k + 1) * hw], gs_ref[...]) for k in range(DIFF_HEADS)], axis=1)
    ups = (_dot(_bf(of_ref[...]), wuf_ref[...]), _dot(_bf(on_ref[...]), wun_ref[...]), _dot(_bf(od), wud_ref[...]))
    merged = None
    for k, up in enumerate(ups):
        cols = slice(k * D_MODEL, (k + 1) * D_MODEL)
        gate = jax.nn.sigmoid(_dot(h, wg_ref[:, cols]) + bg_ref[:, cols])
        merged = gate * up if merged is None else merged + gate * up
    o_ref[...] = x + _dot(_bf(merged), wo_ref[...])


def _post(x, ln1_g, o_fox, o_nsa, o_diff, gs, w_up_fox, w_up_nsa, w_up_diff, w_gate, b_gate, w_o, tm):
    n = x.shape[0]
    row = lambda w: pl.BlockSpec((tm, w), lambda i: (i, 0))
    consts = (gs, w_up_fox, w_up_nsa, w_up_diff, w_gate, b_gate, w_o)
    return pl.pallas_call(
        _post_kernel,
        grid=(n // tm,),
        in_specs=[row(D_MODEL), _full(ln1_g), row(FOX_W), row(NSA_W), row(DIFF_W)] + [_full(a) for a in consts],
        out_specs=row(D_MODEL),
        out_shape=jax.ShapeDtypeStruct((n, D_MODEL), F32),
        compiler_params=_params(("arbitrary",)),
        name="merge_out",
    )(x, ln1_g, o_fox, o_nsa, o_diff, *consts)


def _mlp_kernel(x_ref, pe_ref, g2_ref, w1_ref, w2_ref, gp_ref, wpg_ref, wp_ref, o_ref, *, ff_chunk):
    x = x_ref[...]
    h = _bf(_rms(x, g2_ref[...]))
    acc = None
    for c in range(D_FF // ff_chunk):
        cols = slice(c * ff_chunk, (c + 1) * ff_chunk)
        a = jnp.square(jnp.maximum(_dot(h, w1_ref[:, cols]), 0.0))
        part = _dot(_bf(a), w2_ref[cols, :])
        acc = part if acc is None else acc + part
    x = x + acc
    gate = jax.nn.sigmoid(_dot(_bf(_rms(x, gp_ref[...])), wpg_ref[...]))
    o_ref[...] = x + gate * _dot(_bf(pe_ref[...]), wp_ref[...])


def _mlp_ple(x, pe, ln2_g, w_ff1, w_ff2, ln_ple_g, w_ple_gate, w_ple, tm):
    n = x.shape[0]
    row = lambda w: pl.BlockSpec((tm, w), lambda i: (i, 0))
    consts = (ln2_g, w_ff1, w_ff2, ln_ple_g, w_ple_gate, w_ple)
    return pl.pallas_call(
        functools.partial(_mlp_kernel, ff_chunk=1024),
        grid=(n // tm,),
        in_specs=[row(D_MODEL), row(pe.shape[1])] + [_full(a) for a in consts],
        out_specs=row(D_MODEL),
        out_shape=jax.ShapeDtypeStruct((n, D_MODEL), F32),
        compiler_params=_params(("arbitrary",)),
        name="mlp_ple",
    )(x, pe, *consts)


def _norm_kernel(x_ref, g_ref, o_ref):
    o_ref[...] = _rms(x_ref[...], g_ref[...])


def _final_norm(x, g, tm):
    n = x.shape[0]
    row = pl.BlockSpec((tm, D_MODEL), lambda i: (i, 0))
    return pl.pallas_call(
        _norm_kernel, grid=(n // tm,), in_specs=[row, _full(g)], out_specs=row,
        out_shape=jax.ShapeDtypeStruct((n, D_MODEL), F32),
        compiler_params=_params(("arbitrary",)), name="final_norm")(x, g)


def _rope_table(pos):
    half = HEAD_DIM // 2
    inv = ROPE_THETA ** (-jnp.arange(half, dtype=F32) / half)
    ang = pos.astype(F32)[:, None] * inv[None, :]
    cos, sin = jnp.cos(ang), jnp.sin(ang)
    cos_h = jnp.concatenate([cos, cos], axis=1)
    sin_h = jnp.concatenate([-sin, sin], axis=1)
    one, zero = jnp.ones_like(cos_h), jnp.zeros_like(cos_h)
    return jnp.concatenate([cos_h, cos_h, sin_h, sin_h, cos_h, one, sin_h, zero], axis=1)


def _prep_weights(p):
    segs = jnp.split(p["w_in"], np.cumsum(IN_SIZES)[:-1].tolist(), axis=-1)
    (fq, fk, fv, ff, nq, nkc, nvc, nks, nvs, nkw, nvw, ng, dq, dk, dv) = segs
    pad = jnp.zeros(ff.shape[:-1] + (LANES - FOX_HEADS - 3 * NSA_HEADS,), F32)
    w_re = _bf(jnp.concatenate([fq, fk, fv, nq, nkc, nvc, nks, nvs, nkw, nvw, dq, dk, dv, ff, ng, pad], axis=-1))
    depth = p["w_in"].shape[0]
    b_small = jnp.concatenate([p["b_fox_f"], p["b_nsa_gate"],
                               jnp.zeros((depth, LANES - FOX_HEADS - 3 * NSA_HEADS), F32)], axis=-1)
    pe_cat = jnp.concatenate([p["nsa_pe"][:, 0], p["nsa_pe"][:, 1]], axis=-1)
    z = jnp.zeros_like(p["nsa_phi"][:, 0])
    phi_bd = _bf(jnp.concatenate([jnp.concatenate([p["nsa_phi"][:, 0], z], axis=-1),
                                  jnp.concatenate([z, p["nsa_phi"][:, 1]], axis=-1)], axis=-2))
    lam_init = jnp.asarray([0.8 - 0.6 * math.exp(-0.3 * i) for i in range(depth)], F32)
    out = dict(w_re=w_re, b_small=b_small[:, None, :], pe_cat=pe_cat, phi_bd=phi_bd,
               lam_init=jnp.broadcast_to(lam_init[:, None, None], (depth, 1, LANES)),
               subln_g=(p["diff_subln_g"] * (1.0 - lam_init)[:, None])[:, None, :],
               b_merge_gate=p["b_merge_gate"][:, None, :], diff_lambda=p["diff_lambda"])
    for name in ("w_up_fox", "w_up_nsa", "w_up_diff", "w_merge_gate", "w_o", "w_ff1", "w_ff2", "w_ple", "w_ple_gate"):
        out[name] = _bf(p[name])
    for name in ("ln1_g", "ln2_g", "ln_ple_g"):
        out[name] = p[name][:, None, :]
    return out


def _dense_tail(x, pe, attn, w, i, tm):
    o_fox, o_nsa, o_diff = attn
    x = _post(x, w["ln1_g"][i], o_fox, o_nsa, o_diff, w["subln_g"][i], w["w_up_fox"][i], w["w_up_nsa"][i],
              w["w_up_diff"][i], w["w_merge_gate"][i], w["b_merge_gate"][i], w["w_o"][i], tm)
    return _mlp_ple(x, pe, w["ln2_g"][i], w["w_ff1"][i], w["w_ff2"][i], w["ln_ple_g"][i],
                    w["w_ple_gate"][i], w["w_ple"][i], tm)


def _prompt_layer(x, pe, w, i, tab, b, t, tm, tq):
    fq_b, fkv, fkv_b, nq_b, nsa_new, nsa_b, win, win_b, dq_b, dkv, dkv_b, small = _in_proj(
        x, w["ln1_g"][i], w["w_re"][i], w["b_small"][i], tab, tm)
    r = lambda a: a.reshape(b, t, a.shape[-1])
    logf = small[:, 0:FOX_HEADS]
    c = _cumsum_lanes(jnp.swapaxes(r(logf), 1, 2))
    ck = jnp.swapaxes(c.reshape(b, FOX_HEADS, t // tq, tq), 1, 2)
    cq = jnp.swapaxes(c, 1, 2)
    o_fox = _fox_prompt(r(fq_b), r(fkv_b), ck, cq, tq)
    cb = _compress(nsa_new[None], w["pe_cat"][i:i + 1], w["phi_bd"][i:i + 1], rb=_row_block(b * t))[0]
    o_nsa = _nsa_prompt(r(nq_b), cb.reshape(b, t // NSA_BLOCK, LANES), r(nsa_b), r(win_b), r(small), tq)
    o_diff = _diff_prompt(r(dq_b), r(dkv_b), w["diff_lambda"][i], w["lam_init"][i], tq)
    flat = lambda a: a.reshape(b * t, a.shape[-1])
    x = _dense_tail(x, pe, (flat(o_fox), flat(o_nsa), flat(o_diff)), w, i, tm)
    return x, (r(fkv), r(logf), r(nsa_new), r(win)[:, t - min(NSA_WINDOW, t):], r(dkv))


def _sample_layer(x, pe, w, i, tab, caches, cb_pool, win_state, page_table, past):
    nb = x.shape[0]
    new = _in_proj(x, w["ln1_g"][i], w["w_re"][i], w["b_small"][i], tab, nb)
    fq_b, fkv, fkv_b, nq_b, nsa_new, nsa_b, win, win_b, dq_b, dkv, dkv_b, small = new
    nq16 = _head_rows(nq_b, NSA_HEADS)
    o_cmp, sel = _dec_select(page_table, nq16, cb_pool[i], past)
    o_fox, o_nsa, o_diff = _dec_sweep(
        i, page_table, sel.reshape(nb, LANES), caches, win_state,
        (fq_b, fkv, small, nq16, nsa_new, win, dq_b, dkv), o_cmp, w["diff_lambda"][i], w["lam_init"][i], past)
    o_nsa = o_nsa[:, 0:NSA_HEADS].reshape(nb, NSA_W)
    x = _dense_tail(x, pe, (o_fox[:, 0], o_nsa, o_diff[:, 0]), w, i, nb)
    n_win = win_state.shape[2]
    keep = min(NSA_WINDOW, past + 1)
    new_win = jnp.concatenate([win_state[i][:, n_win + 1 - keep:], win[:, None, :]], axis=1)
    return x, (fkv, small[:, 0:FOX_HEADS], nsa_new, new_win, dkv)


def kernel(x_prompt, x_sample, cache_fox_kv, cache_fox_logf, cache_nsa_kv, state_nsa_win, cache_diff_kv,
           page_table, p_prompt, p_sample,
           ln1_g, w_in, b_fox_f, b_nsa_gate, nsa_pe, nsa_phi, diff_lambda, diff_subln_g,
           w_up_fox, w_up_nsa, w_up_diff, w_merge_gate, b_merge_gate, w_o,
           ln2_g, w_ff1, w_ff2, w_ple, ln_ple_g, w_ple_gate, final_norm_g):
    b, t, d = x_prompt.shape
    nb, ds, _ = x_sample.shape
    depth, n_pool, page, _, _, _ = cache_fox_kv.shape
    n_pages = page_table.shape[1]
    past = n_pages * page
    n_win = state_nsa_win.shape[2]
    assert ds == 1 and page == PAGE_SIZE and n_win == min(NSA_WINDOW, past)
    tm = min(512, b * t)
    tq = 256

    w = _prep_weights(dict(
        w_in=w_in, b_fox_f=b_fox_f, b_nsa_gate=b_nsa_gate, nsa_pe=nsa_pe, nsa_phi=nsa_phi,
        diff_lambda=diff_lambda, diff_subln_g=diff_subln_g, w_up_fox=w_up_fox, w_up_nsa=w_up_nsa,
        w_up_diff=w_up_diff, w_merge_gate=w_merge_gate, b_merge_gate=b_merge_gate, w_o=w_o,
        ln1_g=ln1_g, ln2_g=ln2_g, w_ff1=w_ff1, w_ff2=w_ff2, w_ple=w_ple, ln_ple_g=ln_ple_g,
        w_ple_gate=w_ple_gate))
    tab_p = _rope_table(jnp.arange(t, dtype=jnp.int32))
    tab_s = _rope_table(jnp.full((nb,), past, jnp.int32))

    fox_kv = cache_fox_kv.reshape(depth, n_pool, page, 2 * FOX_W)
    fox_lfT = jnp.pad(jnp.swapaxes(cache_fox_logf, 2, 3), ((0, 0), (0, 0), (0, SUBLANES - FOX_HEADS), (0, 0)))
    nsa_kv = cache_nsa_kv.reshape(depth, n_pool, page, 4 * HEAD_DIM)
    diff_kv = cache_diff_kv.reshape(depth, n_pool, page, 2 * DIFF_W)
    win_state = state_nsa_win.reshape(depth, nb, n_win, 2 * HEAD_DIM)
    cb_pool = _compress(nsa_kv.reshape(depth, n_pool * page, 4 * HEAD_DIM), w["pe_cat"], w["phi_bd"],
                        rb=_row_block(n_pool * page))
    cb_pool = cb_pool.reshape(depth, n_pool, page // NSA_BLOCK, LANES)

    xp = x_prompt.reshape(b * t, d)
    xs = x_sample.reshape(nb, d)
    outs_p, outs_s = [], []
    for i in range(depth):
        xp, new_p = _prompt_layer(xp, p_prompt[i].reshape(b * t, -1), w, i, tab_p, b, t, tm, tq)
        outs_p.append(new_p)
        xs, new_s = _sample_layer(xs, p_sample[i].reshape(nb, -1), w, i, tab_s,
                                  (fox_kv, fox_lfT, nsa_kv, diff_kv), cb_pool, win_state, page_table, past)
        outs_s.append(new_s)
    fn = final_norm_g[None, :]
    y_prompt = _final_norm(xp, fn, tm).reshape(b, t, d)
    y_sample = _final_norm(xs, fn, nb).reshape(nb, 1, d)

    stack = lambda outs, k: jnp.stack([o[k] for o in outs])
    fkv_p = stack(outs_p, 0).reshape(depth, b, t, 2, FOX_HEADS, HEAD_DIM)
    flf_p = stack(outs_p, 1)
    nkv_p = stack(outs_p, 2).reshape(depth, b, t, 4, HEAD_DIM)
    nwin_p = stack(outs_p, 3).reshape(depth, b, -1, 2, HEAD_DIM)
    dkv_p = stack(outs_p, 4).reshape(depth, b, t, 2, DIFF_HEADS, 2 * HEAD_DIM)
    fkv_s = stack(outs_s, 0).reshape(depth, nb, 1, 2, FOX_HEADS, HEAD_DIM)
    flf_s = stack(outs_s, 1).reshape(depth, nb, 1, FOX_HEADS)
    nkv_s = stack(outs_s, 2).reshape(depth, nb, 1, 4, HEAD_DIM)
    nwin_s = stack(outs_s, 3).reshape(depth, nb, -1, 2, HEAD_DIM)
    dkv_s = stack(outs_s, 4).reshape(depth, nb, 1, 2, DIFF_HEADS, 2 * HEAD_DIM)
    return (y_prompt, y_sample, fkv_p, flf_p, nkv_p, nwin_p, dkv_p, fkv_s, flf_s, nkv_s, nwin_s, dkv_s)
```

```python
import functools
import math

import numpy as np
import jax
import jax.numpy as jnp
from jax import lax
from jax.experimental import pallas as pl
from jax.experimental.pallas import tpu as pltpu

F32 = jnp.float32
BF16 = jnp.bfloat16

D_MODEL = 1024
HEAD_DIM = 64
FOX_HEADS = 4
NSA_HEADS = 4
NSA_BLOCK = 64
NSA_TOPK = 16
NSA_WINDOW = 512
DIFF_HEADS = 4
FOX_W = FOX_HEADS * HEAD_DIM
NSA_W = NSA_HEADS * HEAD_DIM
DIFF_W = DIFF_HEADS * 2 * HEAD_DIM
D_FF = 4 * D_MODEL
ROPE_THETA = 10000.0
RMS_EPS = 1e-6
NEG_INF = -1e30
PAGE_SIZE = 128
IN_SIZES = (FOX_W, FOX_W, FOX_W, FOX_HEADS,
            NSA_W, HEAD_DIM, HEAD_DIM, HEAD_DIM, HEAD_DIM, HEAD_DIM, HEAD_DIM, 3 * NSA_HEADS,
            DIFF_W, DIFF_W, DIFF_W)
SCALE = HEAD_DIM ** -0.5
LOG2E = 1.4426950408889634
QSCALE = SCALE * LOG2E
LANES = 128
SUBLANES = 8
VMEM_LIMIT = 56 * 1024 * 1024

C_FQ, C_FKV, C_NQ, C_NSA, C_WIN, C_DQ, C_DK, C_DV, C_SMALL, C_END = (
    0, 256, 768, 1024, 1280, 1408, 1920, 2432, 2944, 3072)


def _dot(a, b):
    return jnp.dot(a, b, preferred_element_type=F32)


def _dot_nt(a, b):
    return lax.dot_general(a, b, (((1,), (1,)), ((), ())), preferred_element_type=F32)


def _bf(x):
    return x.astype(BF16)


def _dot_split3(a, b):
    hi = _bf(a)
    r1 = a - hi.astype(F32)
    mid = _bf(r1)
    lo = _bf(r1 - mid.astype(F32))
    return _dot(hi, b) + _dot(mid, b) + _dot(lo, b)


def _rms(x, g):
    return x * lax.rsqrt(jnp.mean(x * x, axis=-1, keepdims=True) + RMS_EPS) * g


def _params(sem):
    return pltpu.CompilerParams(dimension_semantics=sem, vmem_limit_bytes=VMEM_LIMIT)


def _full(a):
    nd = a.ndim
    return pl.BlockSpec(a.shape, lambda *_: (0,) * nd)


def _rope(y, cos, sin):
    lane = lax.broadcasted_iota(jnp.int32, (1, LANES), 1)
    first_half = (lane % HEAD_DIM) < (HEAD_DIM // 2)
    outs = []
    for c in range(y.shape[1] // LANES):
        blk = y[:, c * LANES:(c + 1) * LANES]
        partner = jnp.where(first_half,
                            pltpu.roll(blk, LANES - HEAD_DIM // 2, 1),
                            pltpu.roll(blk, HEAD_DIM // 2, 1))
        outs.append(blk * cos + partner * sin)
    return outs[0] if len(outs) == 1 else jnp.concatenate(outs, axis=1)


def _in_proj_kernel(x_ref, g_ref, w_ref, b_ref, tab_ref,
                    fq_ref, fkv_ref, fkvb_ref, nq_ref, nsa_ref, nsab_ref, win_ref, winb_ref,
                    dq_ref, dkv_ref, dkvb_ref, small_ref):
    h = _bf(_rms(x_ref[...], g_ref[...]))
    seg = lambda a, b: _dot(h, w_ref[:, a:b])
    cos2, sin2 = tab_ref[:, 0:128], tab_ref[:, 128:256]
    cos1, sin1 = tab_ref[:, 256:384], tab_ref[:, 384:512]

    fq_ref[...] = _bf(seg(C_FQ, C_FKV) * QSCALE)
    y = seg(C_FKV, C_NQ)
    fkv_ref[...] = y
    fkvb_ref[...] = _bf(y)
    nq_ref[...] = _bf(_rope(seg(C_NQ, C_NSA), cos2, sin2) * QSCALE)
    y = _rope(seg(C_NSA, C_WIN), cos1, sin1)
    nsa_ref[...] = y
    nsab_ref[...] = _bf(y)
    y = _rope(seg(C_WIN, C_DQ), cos1, sin1)
    win_ref[...] = y
    winb_ref[...] = _bf(y)
    dq_ref[...] = _bf(_rope(seg(C_DQ, C_DK), cos2, sin2) * QSCALE)
    y = _rope(seg(C_DK, C_DV), cos2, sin2)
    dkv_ref[:, 0:DIFF_W] = y
    dkvb_ref[:, 0:DIFF_W] = _bf(y)
    y = seg(C_DV, C_SMALL)
    dkv_ref[:, DIFF_W:2 * DIFF_W] = y
    dkvb_ref[:, DIFF_W:2 * DIFF_W] = _bf(y)
    z = seg(C_SMALL, C_END) + b_ref[...]
    lane = lax.broadcasted_iota(jnp.int32, (1, LANES), 1)
    log_sig = jnp.minimum(z, 0.0) - jnp.log1p(jnp.exp(-jnp.abs(z)))
    small_ref[...] = jnp.where(lane < FOX_HEADS, log_sig, jax.nn.sigmoid(z))


def _in_proj(x, ln_g, w_re, b_small, tab, tm):
    n = x.shape[0]
    period = tab.shape[0] // tm
    row = lambda w: pl.BlockSpec((tm, w), lambda i: (i, 0))
    widths = (FOX_W, 2 * FOX_W, 2 * FOX_W, NSA_W, 256, 256, 128, 128, DIFF_W, 2 * DIFF_W, 2 * DIFF_W, LANES)
    dtypes = (BF16, F32, BF16, BF16, F32, BF16, F32, BF16, BF16, F32, BF16, F32)
    return pl.pallas_call(
        _in_proj_kernel,
        grid=(n // tm,),
        in_specs=[row(D_MODEL), _full(ln_g), _full(w_re), _full(b_small),
                  pl.BlockSpec((tm, 512), lambda i: (i % period, 0))],
        out_specs=[row(w) for w in widths],
        out_shape=[jax.ShapeDtypeStruct((n, w), d) for w, d in zip(widths, dtypes)],
        compiler_params=_params(("arbitrary",)),
        name="in_proj",
    )(x, ln_g, w_re, b_small, tab)


def _cumsum_kernel(x_ref, o_ref):
    x = x_ref[0]
    t = x.shape[1]
    idx = lax.broadcasted_iota(jnp.int32, x.shape, 1)
    shift = 1
    while shift < t:
        x = x + jnp.where(idx >= shift, pltpu.roll(x, shift, 1), 0.0)
        shift *= 2
    x = x * LOG2E
    hi = _bf(x).astype(F32)
    mid = _bf(x - hi).astype(F32)
    lo = _bf(x - hi - mid).astype(F32)
    o_ref[0, 0] = hi
    o_ref[0, 1] = mid
    o_ref[0, 2] = lo


def _cumsum_parts(x):
    b, h, t = x.shape
    x = jnp.pad(x, ((0, 0), (0, SUBLANES - h), (0, 0)))
    return pl.pallas_call(
        _cumsum_kernel, grid=(b,),
        in_specs=[pl.BlockSpec((1, SUBLANES, t), lambda i: (i, 0, 0))],
        out_specs=pl.BlockSpec((1, 3, SUBLANES, t), lambda i: (i, 0, 0, 0)),
        out_shape=jax.ShapeDtypeStruct((b, 3, SUBLANES, t), F32),
        compiler_params=_params(("arbitrary",)), name="fox_cumsum")(x)[:, :, :h]


def _diff_lambda(dl_ref, li_ref):
    dl = dl_ref[...]
    a = jnp.sum(dl[0:1] * dl[1:2], axis=-1, keepdims=True)
    b = jnp.sum(dl[2:3] * dl[3:4], axis=-1, keepdims=True)
    return jnp.exp(a) - jnp.exp(b) + li_ref[0:1, 0:1]


def _compress_kernel(x_ref, pe_ref, phi_ref, phit_ref, o_ref, ot_ref):
    x = x_ref[0]
    nb = x.shape[0] // NSA_BLOCK
    mean = jnp.sum(x.reshape(nb, NSA_BLOCK, LANES), axis=1) * (1.0 / NSA_BLOCK)
    pe_mean = jnp.sum(pe_ref[0], axis=0, keepdims=True) * (1.0 / NSA_BLOCK)
    mp = _bf(mean + pe_mean)
    o_ref[0] = _dot(mp, phi_ref[0])
    ot_ref[0] = _dot_nt(phit_ref[0], mp)


def _compress_prompt(x, pe_cat, phi_bd, phi_bdt):
    b, t, _ = x.shape
    nb = t // NSA_BLOCK
    return pl.pallas_call(
        _compress_kernel,
        grid=(b,),
        in_specs=[pl.BlockSpec((1, t, LANES), lambda i: (i, 0, 0)), _full(pe_cat), _full(phi_bd), _full(phi_bdt)],
        out_specs=[pl.BlockSpec((1, nb, LANES), lambda i: (i, 0, 0)),
                   pl.BlockSpec((1, LANES, nb), lambda i: (i, 0, 0))],
        out_shape=[jax.ShapeDtypeStruct((b, nb, LANES), F32), jax.ShapeDtypeStruct((b, LANES, nb), F32)],
        compiler_params=_params(("arbitrary",)),
        name="nsa_compress",
    )(x, pe_cat, phi_bd, phi_bdt)


def _compress_pages_kernel(x_ref, pe_ref, phi_ref, o_ref, mean_scr):
    per_page = PAGE_SIZE // NSA_BLOCK
    for pg in range(x_ref.shape[0]):
        rows = x_ref[pg].T
        mean_scr[pg * per_page:(pg + 1) * per_page, :] = (
            jnp.sum(rows.reshape(per_page, NSA_BLOCK, LANES), axis=1) * (1.0 / NSA_BLOCK))
    pe_mean = jnp.sum(pe_ref[0], axis=0, keepdims=True) * (1.0 / NSA_BLOCK)
    o_ref[0] = _dot(_bf(mean_scr[...] + pe_mean), phi_ref[0])


def _compress_pages(x, pe_cat, phi_bd):
    nl, n_pool, _, page = x.shape
    per_page = page // NSA_BLOCK
    pb = 32
    while n_pool % pb:
        pb //= 2
    assert pb * per_page % SUBLANES == 0
    return pl.pallas_call(
        _compress_pages_kernel,
        grid=(nl, n_pool // pb),
        in_specs=[pl.BlockSpec((None, pb, 2 * HEAD_DIM, page), lambda l, i: (l, i, 0, 0)),
                  pl.BlockSpec((1, NSA_BLOCK, LANES), lambda l, i: (l, 0, 0)),
                  pl.BlockSpec((1, LANES, LANES), lambda l, i: (l, 0, 0))],
        out_specs=pl.BlockSpec((1, pb * per_page, LANES), lambda l, i: (l, i, 0)),
        out_shape=jax.ShapeDtypeStruct((nl, n_pool * per_page, LANES), F32),
        scratch_shapes=[pltpu.VMEM((pb * per_page, LANES), F32)],
        compiler_params=_params(("arbitrary", "arbitrary")),
        name="nsa_compress_pages",
    )(x, pe_cat, phi_bd)


def _topk_membership(score, n_valid, n_sel):
    blk = lax.broadcasted_iota(jnp.int32, score.shape, 1)
    rank = jnp.zeros(score.shape, F32)
    for n in range(n_valid):
        col = score[:, n:n + 1]
        beats = (col > score) | ((col == score) & (blk > n))
        rank = rank + beats.astype(F32)
    return ((rank < n_sel) & (blk < n_valid)).astype(F32)


V_ONES_PAD = 16


def _online_update(s, vt_b, m, acc):
    m_new = jnp.maximum(m, jnp.max(s, axis=0, keepdims=True))
    p = jnp.exp2(s - m_new)
    return m_new, jnp.exp2(m - m_new) * acc + _dot(vt_b, _bf(p))


def _online_init(rows, cols):
    return jnp.full((1, cols), NEG_INF, F32), jnp.zeros((rows, cols), F32)


def _online_update_split(s, vts, m, *accs):
    w = s.shape[1] // len(vts)
    m_new = jnp.maximum(m, jnp.max(s, axis=0, keepdims=True))
    p = _bf(jnp.exp2(s - m_new))
    alpha = jnp.exp2(m - m_new)
    new = [alpha[:, k * w:(k + 1) * w] * acc + _dot(vt, p[:, k * w:(k + 1) * w])
           for k, (vt, acc) in enumerate(zip(vts, accs))]
    return (m_new, *new)


def _online_init_split(rows, cols, n):
    return (jnp.full((1, n * cols), NEG_INF, F32),) + (jnp.zeros((rows, cols), F32),) * n


def _block_diag(a, b):
    z = jnp.zeros_like(a)
    return jnp.concatenate([jnp.concatenate([a, z], axis=1), jnp.concatenate([z, b], axis=1)], axis=0)


def _fox_t_kernel(qw_ref, k_ref, vt_ref, o_ref, *, tq):
    i = pl.program_id(1)
    va = HEAD_DIM + V_ONES_PAD
    groups = FOX_HEADS // 2
    wq = [_block_diag(qw_ref[0, (2 * g) * LANES:(2 * g + 1) * LANES, :],
                      qw_ref[0, (2 * g + 1) * LANES:(2 * g + 2) * LANES, :]) for g in range(groups)]
    qpos = i * tq + lax.broadcasted_iota(jnp.int32, (1, 2 * tq), 1) % tq
    krow = lax.broadcasted_iota(jnp.int32, (tq, 1), 0)

    def step(j, carry, diag=False):
        rows = pl.ds(pl.multiple_of(j * tq, tq), tq)
        out = []
        for g in range(groups):
            s = _dot(k_ref[0, rows, 2 * g * LANES:(2 * g + 2) * LANES], wq[g])
            if diag:
                s = jnp.where(j * tq + krow <= qpos, s, NEG_INF)
            vts = [vt_ref[0, j, (2 * g + k) * va:(2 * g + k + 1) * va, :] for k in range(2)]
            out.extend(_online_update_split(s, vts, *carry[3 * g:3 * g + 3]))
        return tuple(out)

    carry = lax.fori_loop(0, i, step, _online_init_split(va, tq, 2) * groups)
    carry = step(i, carry, diag=True)
    accs = [carry[3 * g + 1 + k] for g in range(groups) for k in range(2)]
    o_ref[0] = jnp.concatenate([a[0:HEAD_DIM] / a[HEAD_DIM:HEAD_DIM + 1] for a in accs], axis=0).T


def _fox_prompt_t(qw, kaug, vt, tq):
    b, _, t = qw.shape
    va = vt.shape[2]
    return pl.pallas_call(
        functools.partial(_fox_t_kernel, tq=tq),
        grid=(b, t // tq),
        in_specs=[pl.BlockSpec((1, FOX_HEADS * LANES, tq), lambda bi, i: (bi, 0, i)),
                  pl.BlockSpec((1, t, FOX_HEADS * LANES), lambda bi, i: (bi, 0, 0)),
                  pl.BlockSpec((1, t // tq, va, tq), lambda bi, i: (bi, 0, 0, 0))],
        out_specs=pl.BlockSpec((1, tq, FOX_W), lambda bi, i: (bi, i, 0)),
        out_shape=jax.ShapeDtypeStruct((b, t, FOX_W), F32),
        compiler_params=_params(("arbitrary", "arbitrary")),
        name="fox_prompt",
    )(qw, kaug, vt)


def _diff_t_kernel(qt_ref, k_ref, vt_ref, dl_ref, li_ref, o_ref, *, tq):
    i = pl.program_id(1)
    hw = 2 * HEAD_DIM
    va = hw + V_ONES_PAD
    lam = _diff_lambda(dl_ref, li_ref)
    drow = lax.broadcasted_iota(jnp.int32, (hw, 1), 0)
    groups = DIFF_HEADS // 2

    def head_weights(h):
        qt = qt_ref[0, h * hw:(h + 1) * hw, :].astype(F32)
        return _bf(jnp.concatenate([jnp.where(drow < HEAD_DIM, qt, 0.0),
                                    jnp.where(drow >= HEAD_DIM, qt, 0.0)], axis=1))

    wq = [_block_diag(head_weights(2 * g), head_weights(2 * g + 1)) for g in range(groups)]
    qpos = i * tq + lax.broadcasted_iota(jnp.int32, (1, 4 * tq), 1) % tq
    krow = lax.broadcasted_iota(jnp.int32, (tq, 1), 0)

    def step(j, carry, diag=False):
        rows = pl.ds(pl.multiple_of(j * tq, tq), tq)
        out = []
        for g in range(groups):
            s = _dot(k_ref[0, rows, 2 * g * hw:(2 * g + 2) * hw], wq[g])
            if diag:
                s = jnp.where(j * tq + krow <= qpos, s, NEG_INF)
            vts = [vt_ref[0, j, (2 * g + k) * va:(2 * g + k + 1) * va, :] for k in range(2)]
            out.extend(_online_update_split(s, vts, *carry[3 * g:3 * g + 3]))
        return tuple(out)

    carry = lax.fori_loop(0, i, step, _online_init_split(va, 2 * tq, 2) * groups)
    carry = step(i, carry, diag=True)
    outs = []
    for acc in [carry[3 * g + 1 + k] for g in range(groups) for k in range(2)]:
        o = acc[0:hw] / acc[hw:hw + 1]
        outs.append(o[:, 0:tq] - lam * o[:, tq:2 * tq])
    o_ref[0] = jnp.concatenate(outs, axis=0).T


def _diff_prompt_t(qt, k_b, vt, dl, li, tq):
    b, _, t = qt.shape
    va = vt.shape[2]
    return pl.pallas_call(
        functools.partial(_diff_t_kernel, tq=tq),
        grid=(b, t // tq),
        in_specs=[pl.BlockSpec((1, DIFF_W, tq), lambda bi, i: (bi, 0, i)),
                  pl.BlockSpec((1, t, DIFF_W), lambda bi, i: (bi, 0, 0)),
                  pl.BlockSpec((1, t // tq, va, tq), lambda bi, i: (bi, 0, 0, 0)),
                  _full(dl), _full(li)],
        out_specs=pl.BlockSpec((1, tq, DIFF_W), lambda bi, i: (bi, i, 0)),
        out_shape=jax.ShapeDtypeStruct((b, t, DIFF_W), F32),
        compiler_params=_params(("arbitrary", "arbitrary")),
        name="diff_prompt",
    )(qt, k_b, vt, dl, li)


def _topk_membership_t(score, n_sel):
    nb = score.shape[0]
    blk = lax.broadcasted_iota(jnp.int32, score.shape, 0)
    rank = jnp.zeros(score.shape, F32)
    for n in range(nb):
        row = score[n:n + 1, :]
        beats = (row > score) | ((row == score) & (blk > n))
        rank = rank + beats.astype(F32)
    return (rank < n_sel).astype(F32)


def _nsa_t_kernel(qt_ref, cb_ref, cbt_ref, ks_ref, vst_ref, kw_ref, vwt_ref, gt_ref, o_ref, *, tq, n_sel):
    i = pl.program_id(1)
    nb = cb_ref.shape[1]
    nh = NSA_HEADS
    qt4 = jnp.concatenate([qt_ref[0, h * HEAD_DIM:(h + 1) * HEAD_DIM, :] for h in range(nh)], axis=1)
    qpos = i * tq + lax.broadcasted_iota(jnp.int32, (1, tq), 1)
    qpos4 = i * tq + lax.broadcasted_iota(jnp.int32, (1, nh * tq), 1) % tq
    blk = lax.broadcasted_iota(jnp.int32, (nb, 1), 0)
    krow = lax.broadcasted_iota(jnp.int32, (tq, 1), 0)

    kcb = _bf(cb_ref[0][:, 0:HEAD_DIM])
    vcbt = _bf(cbt_ref[0][HEAD_DIM:2 * HEAD_DIM, :])
    done4 = ((blk + 1) * NSA_BLOCK - 1) <= qpos4
    s = jnp.where(done4, _dot(kcb, qt4), NEG_INF)
    e = jnp.exp2(s - jnp.max(s, axis=0, keepdims=True))
    p_c = e / jnp.sum(e, axis=0, keepdims=True) * done4.astype(F32)
    o_cmp = _dot(vcbt, _bf(p_c))
    imp = p_c[:, 0:tq]
    for h in range(1, nh):
        imp = imp + p_c[:, h * tq:(h + 1) * tq]

    cur = qpos // NSA_BLOCK
    forced = (blk == 0) | (blk == cur) | (blk == cur - 1)
    score = jnp.where(forced, jnp.inf, jnp.where(blk <= cur, imp, -jnp.inf))
    sel = _bf(_topk_membership_t(score, n_sel))
    blk_lane = lax.broadcasted_iota(jnp.int32, (1, nb), 1)
    va = HEAD_DIM + V_ONES_PAD

    def sel_step(j, carry, diag=False):
        rows = pl.ds(pl.multiple_of(j * tq, tq), tq)
        kpos = j * tq + krow
        expand = _bf((kpos // NSA_BLOCK == blk_lane).astype(F32))
        chosen = _dot(expand, sel)
        if diag:
            chosen = jnp.where(kpos <= qpos, chosen, 0.0)
        chosen4 = jnp.concatenate([chosen] * nh, axis=1)
        s = jnp.where(chosen4 > 0.5, _dot(ks_ref[0, rows, 0:HEAD_DIM], qt4), NEG_INF)
        return _online_update(s, vst_ref[0, j], *carry)

    carry = sel_step(i, _online_init(va, nh * tq), diag=True)
    _, acc = lax.fori_loop(0, i, sel_step, carry)
    o_sel = acc[0:HEAD_DIM] / acc[HEAD_DIM:HEAD_DIM + 1]

    carry = _online_init(va, nh * tq)
    for d in range(NSA_WINDOW // tq + 1):
        j = jnp.maximum(i - d, 0)
        rows = pl.ds(pl.multiple_of(j * tq, tq), tq)
        dist = qpos4 - (j * tq + krow) + jnp.where(i - d >= 0, 0, 2 * NSA_WINDOW + tq)
        ok = (dist >= 0) & (dist <= NSA_WINDOW)
        s = jnp.where(ok, _dot(kw_ref[0, rows, 0:HEAD_DIM], qt4), NEG_INF)
        carry = _online_update(s, vwt_ref[0, j], *carry)
    o_win = carry[1][0:HEAD_DIM] / carry[1][HEAD_DIM:HEAD_DIM + 1]

    gt = gt_ref[0]
    outs = []
    for h in range(nh):
        cols = slice(h * tq, (h + 1) * tq)
        r0 = FOX_HEADS + 3 * h
        outs.append(gt[r0:r0 + 1] * o_cmp[:, cols] + gt[r0 + 1:r0 + 2] * o_sel[:, cols]
                    + gt[r0 + 2:r0 + 3] * o_win[:, cols])
    o_ref[0] = jnp.concatenate(outs, axis=0).T


def _nsa_prompt_t(qt, cb, cbt, nsa_b, vst, win_b, vwt, small_t, tq):
    b, _, t = qt.shape
    nb = t // NSA_BLOCK
    va = vst.shape[2]
    chunks = pl.BlockSpec((1, t // tq, va, tq), lambda bi, i: (bi, 0, 0, 0))
    return pl.pallas_call(
        functools.partial(_nsa_t_kernel, tq=tq, n_sel=min(NSA_TOPK, nb)),
        grid=(b, t // tq),
        in_specs=[pl.BlockSpec((1, NSA_W, tq), lambda bi, i: (bi, 0, i)),
                  pl.BlockSpec((1, nb, LANES), lambda bi, i: (bi, 0, 0)),
                  pl.BlockSpec((1, LANES, nb), lambda bi, i: (bi, 0, 0)),
                  pl.BlockSpec((1, t, LANES), lambda bi, i: (bi, 0, 1)),
                  chunks,
                  pl.BlockSpec((1, t, LANES), lambda bi, i: (bi, 0, 0)),
                  chunks,
                  pl.BlockSpec((1, LANES, tq), lambda bi, i: (bi, 0, i))],
        out_specs=pl.BlockSpec((1, tq, NSA_W), lambda bi, i: (bi, i, 0)),
        out_shape=jax.ShapeDtypeStruct((b, t, NSA_W), F32),
        compiler_params=_params(("arbitrary", "arbitrary")),
        name="nsa_prompt",
    )(qt, cb, cbt, nsa_b, vst, win_b, vwt, small_t)


def _vt_chunks(v, tq, heads):
    b, t, w = v.shape
    dv = w // heads
    vt = jnp.swapaxes(v.reshape(b, t // tq, tq, heads, dv), 2, 4)
    vt = jnp.swapaxes(vt, 2, 3)
    ones = jnp.ones((b, t // tq, heads, 1, tq), v.dtype)
    pad = jnp.zeros((b, t // tq, heads, V_ONES_PAD - 1, tq), v.dtype)
    return jnp.concatenate([vt, ones, pad], axis=3).reshape(b, t // tq, heads * (dv + V_ONES_PAD), tq)


DEC_ROWS = 16


def _head_rows(x, n_heads):
    n = x.shape[0]
    return jnp.pad(x.reshape(n, n_heads, HEAD_DIM), ((0, 0), (0, DEC_ROWS - n_heads), (0, 0)))


def _dec_select_kernel(pt_ref, q_ref, cbp_ref, ocmp_ref, sel_ref, cb_scr, *, n_pages, past, n_sel):
    b = pl.program_id(0)
    n_cmp = (past + 1) // NSA_BLOCK
    n_blk = -(-(past + 1) // NSA_BLOCK)
    per_page = PAGE_SIZE // NSA_BLOCK
    cb_scr[...] = jnp.zeros(cb_scr.shape, F32)
    for p in range(n_pages):
        cb_scr[p * per_page:(p + 1) * per_page, :] = cbp_ref[pt_ref[b, p]]
    cb = cb_scr[...]
    kcb, vcb = _bf(cb[:, 0:HEAD_DIM]), _bf(cb[:, HEAD_DIM:])
    blk = lax.broadcasted_iota(jnp.int32, (1, LANES), 1)
    done = (blk < n_cmp) & (((blk + 1) * NSA_BLOCK - 1) <= past)
    s = jnp.where(done, _dot_nt(q_ref[0], kcb), NEG_INF)
    e = jnp.exp2(s - jnp.max(s, axis=-1, keepdims=True))
    p_c = e / jnp.sum(e, axis=-1, keepdims=True) * done.astype(F32)
    ocmp_ref[0] = _dot(_bf(p_c), vcb)
    head_row = lax.broadcasted_iota(jnp.int32, (DEC_ROWS, 1), 0) < NSA_HEADS
    imp = jnp.sum(jnp.where(head_row, p_c, 0.0), axis=0, keepdims=True)
    imp = jnp.where(blk < n_cmp, imp, -jnp.inf)
    cur = past // NSA_BLOCK
    forced = (blk == 0) | (blk == cur) | (blk == cur - 1)
    score = jnp.where(forced, jnp.inf, jnp.where(blk <= cur, imp, -jnp.inf))
    sel_ref[0] = _topk_membership(score, n_blk, n_sel).astype(jnp.int32)


def _dec_select(page_table, nq16, cb_pool, past):
    nb, n_pages = page_table.shape
    n_blk = -(-(past + 1) // NSA_BLOCK)
    assert n_pages * (PAGE_SIZE // NSA_BLOCK) <= LANES and n_blk <= LANES
    kern = functools.partial(_dec_select_kernel, n_pages=n_pages, past=past, n_sel=min(NSA_TOPK, n_blk))
    return pl.pallas_call(
        kern,
        grid_spec=pltpu.PrefetchScalarGridSpec(
            num_scalar_prefetch=1, grid=(nb,),
            in_specs=[pl.BlockSpec((1, DEC_ROWS, HEAD_DIM), lambda b, pt: (b, 0, 0)),
                      pl.BlockSpec(cb_pool.shape, lambda b, pt: (0, 0, 0))],
            out_specs=[pl.BlockSpec((1, DEC_ROWS, HEAD_DIM), lambda b, pt: (b, 0, 0)),
                       pl.BlockSpec((1, 1, LANES), lambda b, pt: (b, 0, 0))],
            scratch_shapes=[pltpu.VMEM((LANES, LANES), F32)]),
        out_shape=[jax.ShapeDtypeStruct((nb, DEC_ROWS, HEAD_DIM), F32),
                   jax.ShapeDtypeStruct((nb, 1, LANES), jnp.int32)],
        compiler_params=_params(("arbitrary",)),
        name="nsa_dec_select",
    )(page_table, nq16, cb_pool)


def _pick_lane_per_row(x_row, lane_of_row):
    lane = lax.broadcasted_iota(jnp.int32, (lane_of_row.shape[0], LANES), 1)
    return jnp.sum(jnp.where(lane == lane_of_row, x_row, 0.0), axis=1, keepdims=True)


def _masked_rows(x_row, n_rows, group):
    w = x_row.shape[1]
    lane = lax.broadcasted_iota(jnp.int32, (n_rows, w), 1)
    row = lax.broadcasted_iota(jnp.int32, (n_rows, w), 0)
    return _bf(jnp.where(lane // group == row, x_row.astype(F32), 0.0))


def _suffix_sum_exclusive(x):
    n = x.shape[1]
    idx = lax.broadcasted_iota(jnp.int32, x.shape, 1)
    y = x
    shift = 1
    while shift < n:
        y = y + jnp.where(idx + shift < n, pltpu.roll(y, n - shift, 1), 0.0)
        shift *= 2
    return y - x


def _softmax_with_new(s, s_new):
    m = jnp.maximum(jnp.max(s, axis=-1, keepdims=True), s_new)
    p = jnp.exp2(s - m)
    p_new = jnp.exp2(s_new - m)
    return _bf(p), _bf(p_new).astype(F32), jnp.sum(p, axis=-1, keepdims=True) + p_new


def _dec_attn_kernel(pt_ref, sel_ref, *refs, n_pages, past):
    fox_refs, lf_refs, nsa_refs, diff_refs = (refs[k * n_pages:(k + 1) * n_pages] for k in range(4))
    (win_ref, fq_ref, fnew_ref, small_ref, nq_ref, nnew_ref, wnew_ref, dq_ref, dnew_ref,
     ocmp_ref, dl_ref, li_ref, ofox_ref, onsa_ref, odiff_ref) = refs[4 * n_pages:]
    b = pl.program_id(0)
    row8 = lax.broadcasted_iota(jnp.int32, (SUBLANES, 1), 0)
    row16 = lax.broadcasted_iota(jnp.int32, (DEC_ROWS, 1), 0)
    lane = lax.broadcasted_iota(jnp.int32, (1, LANES), 1)
    hw = 2 * HEAD_DIM
    pages = range(n_pages)
    chunk = lambda x, pg: x[:, pg * PAGE_SIZE:(pg + 1) * PAGE_SIZE]
    small = small_ref[0]

    fq8 = _masked_rows(fq_ref[0], SUBLANES, HEAD_DIM)
    fnew = fnew_ref[0]
    k_new, v_new = _bf(fnew[:, 0:FOX_W]).astype(F32), _bf(fnew[:, FOX_W:]).astype(F32)
    s = jnp.concatenate([_dot(fq8, _bf(fox_refs[pg][0:FOX_W, :])) for pg in pages], axis=1)
    lf = jnp.concatenate([lf_refs[pg][...] for pg in pages], axis=1)
    s = s + (_suffix_sum_exclusive(lf) + _pick_lane_per_row(small, row8)) * LOG2E
    s_new = jnp.sum(fq8.astype(F32) * k_new, axis=1, keepdims=True)
    p, p_new, l = _softmax_with_new(s, s_new)
    acc = p_new * v_new
    for pg in pages:
        acc = acc + _dot_nt(chunk(p, pg), _bf(fox_refs[pg][FOX_W:2 * FOX_W, :]))
    lane_f = lax.broadcasted_iota(jnp.int32, (SUBLANES, FOX_W), 1)
    ofox_ref[0] = jnp.sum(jnp.where((lane_f // HEAD_DIM) == row8, acc / l, 0.0), axis=0, keepdims=True)

    nq = nq_ref[0]
    nnew = nnew_ref[0]
    per_page = PAGE_SIZE // NSA_BLOCK
    parts = []
    for pg in pages:
        chosen = jnp.where(lane < NSA_BLOCK, sel_ref[b, pg * per_page], sel_ref[b, pg * per_page + 1]) > 0
        parts.append(jnp.where(chosen, _dot(nq, _bf(nsa_refs[pg][0:HEAD_DIM, :])), NEG_INF))
    s = jnp.concatenate(parts, axis=1)
    s_new = jnp.sum(nq.astype(F32) * _bf(nnew[:, 2 * HEAD_DIM:3 * HEAD_DIM]).astype(F32), axis=1, keepdims=True)
    p, p_new, l = _softmax_with_new(s, s_new)
    acc = p_new * _bf(nnew[:, 3 * HEAD_DIM:]).astype(F32)
    for pg in pages:
        acc = acc + _dot_nt(chunk(p, pg), _bf(nsa_refs[pg][HEAD_DIM:2 * HEAD_DIM, :]))
    o_sel = acc / l

    win = win_ref[0]
    wnew = wnew_ref[0]
    n_win = win.shape[1]
    pos = past - n_win + lax.broadcasted_iota(jnp.int32, (1, n_win), 1)
    ok = (past - pos <= NSA_WINDOW) & (pos >= 0)
    s = jnp.where(ok, _dot(nq, _bf(win[0:HEAD_DIM, :])), NEG_INF)
    s_new = jnp.sum(nq.astype(F32) * _bf(wnew[:, 0:HEAD_DIM]).astype(F32), axis=1, keepdims=True)
    p, p_new, l = _softmax_with_new(s, s_new)
    o_win = (_dot_nt(p, _bf(win[HEAD_DIM:, :])) + p_new * _bf(wnew[:, HEAD_DIM:]).astype(F32)) / l
    g = [_pick_lane_per_row(small, FOX_HEADS + 3 * row16 + c) for c in range(3)]
    onsa_ref[0] = g[0] * ocmp_ref[0] + g[1] * o_sel + g[2] * o_win

    lam = _diff_lambda(dl_ref, li_ref)
    dq = dq_ref[0].astype(F32)
    dnew = dnew_ref[0]
    lane_h = lax.broadcasted_iota(jnp.int32, (SUBLANES, hw), 1)
    for h in range(DIFF_HEADS):
        cols = slice(h * hw, (h + 1) * hw)
        q2 = _bf(jnp.where((lane_h // HEAD_DIM) == row8, dq[:, cols], 0.0))
        k_rows = pl.ds(h, PAGE_SIZE, stride=2 * DIFF_HEADS)
        v_rows = pl.ds(DIFF_HEADS + h, PAGE_SIZE, stride=2 * DIFF_HEADS)
        s = jnp.concatenate([_dot_nt(q2, _bf(diff_refs[pg][k_rows, :])) for pg in pages], axis=1)
        s_new = jnp.sum(q2.astype(F32) * _bf(dnew[:, cols]).astype(F32), axis=1, keepdims=True)
        p, p_new, l = _softmax_with_new(s, s_new)
        acc = p_new * _bf(dnew[:, DIFF_W + h * hw:DIFF_W + (h + 1) * hw]).astype(F32)
        for pg in pages:
            acc = acc + _dot(chunk(p, pg), _bf(diff_refs[pg][v_rows, :]))
        o = acc / l
        odiff_ref[0, :, cols] = o[0:1] - lam * o[1:2]


def _dec_attn(layer, page_table, sel, caches, win_state, new, o_cmp, dl, li, past):
    nb, n_pages = page_table.shape
    fox_kv, fox_lf, nsa_kv, diff_kv = caches
    fq_b, fkv, small, nq16, nsa_new, win_new, dq_b, dkv = new
    n_win = win_state.shape[3]
    per_seq = lambda w: pl.BlockSpec((1, 1, w), lambda b, pt, sl: (b, 0, 0))
    head_rows = pl.BlockSpec((1, DEC_ROWS, HEAD_DIM), lambda b, pt, sl: (b, 0, 0))
    r3 = lambda a: a.reshape(nb, 1, a.shape[-1])

    def paged(rows, row_block):
        return [pl.BlockSpec((None, None, rows, LANES),
                             lambda b, pt, sl, pg=pg: (layer, pt[b, pg], row_block, 0)) for pg in range(n_pages)]

    kern = functools.partial(_dec_attn_kernel, n_pages=n_pages, past=past)
    return pl.pallas_call(
        kern,
        grid_spec=pltpu.PrefetchScalarGridSpec(
            num_scalar_prefetch=2, grid=(nb,),
            in_specs=(paged(2 * FOX_W, 0) + paged(SUBLANES, 0) + paged(2 * HEAD_DIM, 1)
                      + paged(PAGE_SIZE * 2 * DIFF_HEADS, 0) + [
                pl.BlockSpec((None, 1, 2 * HEAD_DIM, n_win), lambda b, pt, sl: (layer, b, 0, 0)),
                per_seq(FOX_W), per_seq(2 * FOX_W), per_seq(LANES), head_rows, per_seq(256),
                per_seq(LANES), per_seq(DIFF_W), per_seq(2 * DIFF_W), head_rows,
                pl.BlockSpec(dl.shape, lambda b, pt, sl: (0, 0)),
                pl.BlockSpec(li.shape, lambda b, pt, sl: (0, 0)),
            ]),
            out_specs=[per_seq(FOX_W), head_rows, per_seq(DIFF_W)]),
        out_shape=[jax.ShapeDtypeStruct((nb, 1, FOX_W), F32),
                   jax.ShapeDtypeStruct((nb, DEC_ROWS, HEAD_DIM), F32),
                   jax.ShapeDtypeStruct((nb, 1, DIFF_W), F32)],
        compiler_params=_params(("arbitrary",)),
        name="dec_attn",
    )(page_table, sel, *([fox_kv] * n_pages + [fox_lf] * n_pages + [nsa_kv] * n_pages + [diff_kv] * n_pages),
      win_state, r3(fq_b), r3(fkv), r3(small), nq16, r3(nsa_new), r3(win_new), r3(dq_b), r3(dkv),
      o_cmp, dl, li)


def _post_kernel(x_ref, g1_ref, of_ref, on_ref, od_ref, gs_ref, wuf_ref, wun_ref, wud_ref,
                 wg_ref, bg_ref, wo_ref, o_ref):
    x = x_ref[...]
    h = _bf(_rms(x, g1_ref[...]))
    hw = 2 * HEAD_DIM
    od = od_ref[...]
    od = jnp.concatenate([_rms(od[:, k * hw:(k + 1) * hw], gs_ref[...]) for k in range(DIFF_HEADS)], axis=1)
    ups = (_dot(_bf(of_ref[...]), wuf_ref[...]), _dot(_bf(on_ref[...]), wun_ref[...]), _dot(_bf(od), wud_ref[...]))
    merged = None
    for k, up in enumerate(ups):
        cols = slice(k * D_MODEL, (k + 1) * D_MODEL)
        gate = jax.nn.sigmoid(_dot(h, wg_ref[:, cols]) + bg_ref[:, cols])
        merged = gate * up if merged is None else merged + gate * up
    o_ref[...] = x + _dot(_bf(merged), wo_ref[...])


def _post(x, ln1_g, o_fox, o_nsa, o_diff, gs, w_up_fox, w_up_nsa, w_up_diff, w_gate, b_gate, w_o, tm):
    n = x.shape[0]
    row = lambda w: pl.BlockSpec((tm, w), lambda i: (i, 0))
    consts = (gs, w_up_fox, w_up_nsa, w_up_diff, w_gate, b_gate, w_o)
    return pl.pallas_call(
        _post_kernel,
        grid=(n // tm,),
        in_specs=[row(D_MODEL), _full(ln1_g), row(FOX_W), row(NSA_W), row(DIFF_W)] + [_full(a) for a in consts],
        out_specs=row(D_MODEL),
        out_shape=jax.ShapeDtypeStruct((n, D_MODEL), F32),
        compiler_params=_params(("arbitrary",)),
        name="merge_out",
    )(x, ln1_g, o_fox, o_nsa, o_diff, *consts)


def _mlp_kernel(x_ref, pe_ref, g2_ref, w1_ref, w2_ref, gp_ref, wpg_ref, wp_ref, o_ref, *, ff_chunk):
    x = x_ref[...]
    h = _bf(_rms(x, g2_ref[...]))
    acc = None
    for c in range(D_FF // ff_chunk):
        cols = slice(c * ff_chunk, (c + 1) * ff_chunk)
        a = jnp.square(jnp.maximum(_dot(h, w1_ref[:, cols]), 0.0))
        part = _dot(_bf(a), w2_ref[cols, :])
        acc = part if acc is None else acc + part
    x = x + acc
    gate = jax.nn.sigmoid(_dot(_bf(_rms(x, gp_ref[...])), wpg_ref[...]))
    o_ref[...] = x + gate * _dot(_bf(pe_ref[...]), wp_ref[...])


def _mlp_ple(x, pe, ln2_g, w_ff1, w_ff2, ln_ple_g, w_ple_gate, w_ple, tm):
    n = x.shape[0]
    row = lambda w: pl.BlockSpec((tm, w), lambda i: (i, 0))
    consts = (ln2_g, w_ff1, w_ff2, ln_ple_g, w_ple_gate, w_ple)
    return pl.pallas_call(
        functools.partial(_mlp_kernel, ff_chunk=1024),
        grid=(n // tm,),
        in_specs=[row(D_MODEL), row(pe.shape[1])] + [_full(a) for a in consts],
        out_specs=row(D_MODEL),
        out_shape=jax.ShapeDtypeStruct((n, D_MODEL), F32),
        compiler_params=_params(("arbitrary",)),
        name="mlp_ple",
    )(x, pe, *consts)


def _norm_kernel(x_ref, g_ref, o_ref):
    o_ref[...] = _rms(x_ref[...], g_ref[...])


def _final_norm(x, g, tm):
    n = x.shape[0]
    row = pl.BlockSpec((tm, D_MODEL), lambda i: (i, 0))
    return pl.pallas_call(
        _norm_kernel, grid=(n // tm,), in_specs=[row, _full(g)], out_specs=row,
        out_shape=jax.ShapeDtypeStruct((n, D_MODEL), F32),
        compiler_params=_params(("arbitrary",)), name="final_norm")(x, g)


def _rope_table(pos):
    half = HEAD_DIM // 2
    inv = ROPE_THETA ** (-jnp.arange(half, dtype=F32) / half)
    ang = pos.astype(F32)[:, None] * inv[None, :]
    cos, sin = jnp.cos(ang), jnp.sin(ang)
    cos_h = jnp.concatenate([cos, cos], axis=1)
    sin_h = jnp.concatenate([-sin, sin], axis=1)
    one, zero = jnp.ones_like(cos_h), jnp.zeros_like(cos_h)
    return jnp.concatenate([cos_h, cos_h, sin_h, sin_h, cos_h, one, sin_h, zero], axis=1)


def _prep_weights(p):
    segs = jnp.split(p["w_in"], np.cumsum(IN_SIZES)[:-1].tolist(), axis=-1)
    (fq, fk, fv, ff, nq, nkc, nvc, nks, nvs, nkw, nvw, ng, dq, dk, dv) = segs
    pad = jnp.zeros(ff.shape[:-1] + (LANES - FOX_HEADS - 3 * NSA_HEADS,), F32)
    w_re = _bf(jnp.concatenate([fq, fk, fv, nq, nkc, nvc, nks, nvs, nkw, nvw, dq, dk, dv, ff, ng, pad], axis=-1))
    depth = p["w_in"].shape[0]
    b_small = jnp.concatenate([p["b_fox_f"], p["b_nsa_gate"],
                               jnp.zeros((depth, LANES - FOX_HEADS - 3 * NSA_HEADS), F32)], axis=-1)
    pe_cat = jnp.concatenate([p["nsa_pe"][:, 0], p["nsa_pe"][:, 1]], axis=-1)
    z = jnp.zeros_like(p["nsa_phi"][:, 0])
    phi_bd = _bf(jnp.concatenate([jnp.concatenate([p["nsa_phi"][:, 0], z], axis=-1),
                                  jnp.concatenate([z, p["nsa_phi"][:, 1]], axis=-1)], axis=-2))
    lam_init = jnp.asarray([0.8 - 0.6 * math.exp(-0.3 * i) for i in range(depth)], F32)
    out = dict(w_re=w_re, b_small=b_small[:, None, :], pe_cat=pe_cat, phi_bd=phi_bd,
               phi_bdt=jnp.swapaxes(phi_bd, 1, 2),
               lam_init=jnp.broadcast_to(lam_init[:, None, None], (depth, 1, LANES)),
               subln_g=(p["diff_subln_g"] * (1.0 - lam_init)[:, None])[:, None, :],
               b_merge_gate=p["b_merge_gate"][:, None, :], diff_lambda=p["diff_lambda"])
    for name in ("w_up_fox", "w_up_nsa", "w_up_diff", "w_merge_gate", "w_o", "w_ff1", "w_ff2", "w_ple", "w_ple_gate"):
        out[name] = _bf(p[name])
    for name in ("ln1_g", "ln2_g", "ln_ple_g"):
        out[name] = p[name][:, None, :]
    return out


def _dense_tail(x, pe, attn, w, i, tm):
    o_fox, o_nsa, o_diff = attn
    x = _post(x, w["ln1_g"][i], o_fox, o_nsa, o_diff, w["subln_g"][i], w["w_up_fox"][i], w["w_up_nsa"][i],
              w["w_up_diff"][i], w["w_merge_gate"][i], w["b_merge_gate"][i], w["w_o"][i], tm)
    return _mlp_ple(x, pe, w["ln2_g"][i], w["w_ff1"][i], w["w_ff2"][i], w["ln_ple_g"][i],
                    w["w_ple_gate"][i], w["w_ple"][i], tm)


def _prompt_layer(x, pe, w, i, tab, b, t, tm, tq):
    fq_b, fkv, fkv_b, nq_b, nsa_new, nsa_b, win, win_b, dq_b, dkv, dkv_b, small = _in_proj(
        x, w["ln1_g"][i], w["w_re"][i], w["b_small"][i], tab, tm)
    r = lambda a: a.reshape(b, t, a.shape[-1])
    logf = small[:, 0:FOX_HEADS]

    parts = jnp.transpose(_bf(_cumsum_parts(jnp.swapaxes(r(logf), 1, 2))), (0, 3, 2, 1))
    ones3 = jnp.ones((b, t, FOX_HEADS, 3), BF16)
    zpad = jnp.zeros((b, t, FOX_HEADS, LANES - HEAD_DIM - 6), BF16)
    fk = r(fkv_b)[:, :, 0:FOX_W].reshape(b, t, FOX_HEADS, HEAD_DIM)
    fq = r(fq_b).reshape(b, t, FOX_HEADS, HEAD_DIM)
    kaug = jnp.concatenate([fk, -parts, ones3, zpad], axis=-1).reshape(b, t, FOX_HEADS * LANES)
    qaug = jnp.concatenate([fq, ones3, parts, zpad], axis=-1)
    qw = jnp.transpose(qaug, (0, 2, 3, 1)).reshape(b, FOX_HEADS * LANES, t)
    o_fox = _fox_prompt_t(qw, kaug, _vt_chunks(r(fkv_b)[:, :, FOX_W:], tq, FOX_HEADS), tq)

    cb, cbt = _compress_prompt(r(nsa_new), w["pe_cat"][i:i + 1], w["phi_bd"][i:i + 1], w["phi_bdt"][i:i + 1])
    o_nsa = _nsa_prompt_t(jnp.swapaxes(r(nq_b), 1, 2), cb, cbt,
                          r(nsa_b), _vt_chunks(r(nsa_b)[:, :, 3 * HEAD_DIM:], tq, 1),
                          r(win_b), _vt_chunks(r(win_b)[:, :, HEAD_DIM:], tq, 1),
                          jnp.swapaxes(r(small), 1, 2), tq)
    o_diff = _diff_prompt_t(jnp.swapaxes(r(dq_b), 1, 2), r(dkv_b),
                            _vt_chunks(r(dkv_b)[:, :, DIFF_W:], tq, DIFF_HEADS),
                            w["diff_lambda"][i], w["lam_init"][i], tq)
    flat = lambda a: a.reshape(b * t, a.shape[-1])
    x = _dense_tail(x, pe, (flat(o_fox), flat(o_nsa), flat(o_diff)), w, i, tm)
    return x, (r(fkv), r(logf), r(nsa_new), r(win)[:, t - min(NSA_WINDOW, t):], r(dkv))


def _cache_views(cache_fox_kv, cache_fox_logf, cache_nsa_kv, cache_diff_kv, state_nsa_win):
    depth, n_pool, page = cache_fox_kv.shape[:3]
    nb, n_win = state_nsa_win.shape[1:3]
    fox_kv = jnp.transpose(cache_fox_kv, (0, 1, 3, 4, 5, 2)).reshape(depth, n_pool, 2 * FOX_W, page)
    fox_lf = jnp.pad(jnp.swapaxes(cache_fox_logf, 2, 3), ((0, 0), (0, 0), (0, SUBLANES - FOX_HEADS), (0, 0)))
    nsa_kv = jnp.transpose(cache_nsa_kv, (0, 1, 3, 4, 2)).reshape(depth, n_pool, 4 * HEAD_DIM, page)
    diff_kv = cache_diff_kv.reshape(depth, n_pool, page * 2 * DIFF_HEADS, 2 * HEAD_DIM)
    win_state = jnp.transpose(state_nsa_win, (0, 1, 3, 4, 2)).reshape(depth, nb, 2 * HEAD_DIM, n_win)
    win_rows = state_nsa_win.reshape(depth, nb, n_win, 2 * HEAD_DIM)
    return (fox_kv, fox_lf, nsa_kv, diff_kv), win_state, win_rows


def _sample_layer(x, pe, w, i, tab, caches, cb_pool, win_state, win_rows, page_table, past):
    nb = x.shape[0]
    new = _in_proj(x, w["ln1_g"][i], w["w_re"][i], w["b_small"][i], tab, nb)
    fq_b, fkv, fkv_b, nq_b, nsa_new, nsa_b, win, win_b, dq_b, dkv, dkv_b, small = new
    nq16 = _head_rows(nq_b, NSA_HEADS)
    o_cmp, sel = _dec_select(page_table, nq16, cb_pool[i], past)
    o_fox, o_nsa, o_diff = _dec_attn(
        i, page_table, sel.reshape(nb, LANES), caches, win_state,
        (fq_b, fkv, small, nq16, nsa_new, win, dq_b, dkv), o_cmp, w["diff_lambda"][i], w["lam_init"][i], past)
    o_nsa = o_nsa[:, 0:NSA_HEADS].reshape(nb, NSA_W)
    x = _dense_tail(x, pe, (o_fox[:, 0], o_nsa, o_diff[:, 0]), w, i, nb)
    n_win = win_rows.shape[2]
    keep = min(NSA_WINDOW, past + 1)
    new_win = jnp.concatenate([win_rows[i][:, n_win + 1 - keep:], win[:, None, :]], axis=1)
    return x, (fkv, small[:, 0:FOX_HEADS], nsa_new, new_win, dkv)


def kernel(x_prompt, x_sample, cache_fox_kv, cache_fox_logf, cache_nsa_kv, state_nsa_win, cache_diff_kv,
           page_table, p_prompt, p_sample,
           ln1_g, w_in, b_fox_f, b_nsa_gate, nsa_pe, nsa_phi, diff_lambda, diff_subln_g,
           w_up_fox, w_up_nsa, w_up_diff, w_merge_gate, b_merge_gate, w_o,
           ln2_g, w_ff1, w_ff2, w_ple, ln_ple_g, w_ple_gate, final_norm_g):
    b, t, d = x_prompt.shape
    nb, ds, _ = x_sample.shape
    depth, n_pool, page, _, _, _ = cache_fox_kv.shape
    n_pages = page_table.shape[1]
    past = n_pages * page
    n_win = state_nsa_win.shape[2]
    assert ds == 1 and page == PAGE_SIZE and n_win == min(NSA_WINDOW, past)
    tm = min(512, b * t)
    tq = 256

    w = _prep_weights(dict(
        w_in=w_in, b_fox_f=b_fox_f, b_nsa_gate=b_nsa_gate, nsa_pe=nsa_pe, nsa_phi=nsa_phi,
        diff_lambda=diff_lambda, diff_subln_g=diff_subln_g, w_up_fox=w_up_fox, w_up_nsa=w_up_nsa,
        w_up_diff=w_up_diff, w_merge_gate=w_merge_gate, b_merge_gate=b_merge_gate, w_o=w_o,
        ln1_g=ln1_g, ln2_g=ln2_g, w_ff1=w_ff1, w_ff2=w_ff2, w_ple=w_ple, ln_ple_g=ln_ple_g,
        w_ple_gate=w_ple_gate))
    tab_p = _rope_table(jnp.arange(t, dtype=jnp.int32))
    tab_s = _rope_table(jnp.full((nb,), past, jnp.int32))

    caches, win_state, win_rows = _cache_views(cache_fox_kv, cache_fox_logf, cache_nsa_kv, cache_diff_kv,
                                               state_nsa_win)
    cb_pool = _compress_pages(caches[2], w["pe_cat"], w["phi_bd"])
    cb_pool = cb_pool.reshape(depth, n_pool, page // NSA_BLOCK, LANES)

    xp = x_prompt.reshape(b * t, d)
    xs = x_sample.reshape(nb, d)
    outs_p, outs_s = [], []
    for i in range(depth):
        xp, new_p = _prompt_layer(xp, p_prompt[i].reshape(b * t, -1), w, i, tab_p, b, t, tm, tq)
        outs_p.append(new_p)
        xs, new_s = _sample_layer(xs, p_sample[i].reshape(nb, -1), w, i, tab_s,
                                  caches, cb_pool, win_state, win_rows, page_table, past)
        outs_s.append(new_s)
    fn = final_norm_g[None, :]
    y_prompt = _final_norm(xp, fn, tm).reshape(b, t, d)
    y_sample = _final_norm(xs, fn, nb).reshape(nb, 1, d)

    stack = lambda outs, k: jnp.stack([o[k] for o in outs])
    fkv_p = stack(outs_p, 0).reshape(depth, b, t, 2, FOX_HEADS, HEAD_DIM)
    flf_p = stack(outs_p, 1)
    nkv_p = stack(outs_p, 2).reshape(depth, b, t, 4, HEAD_DIM)
    nwin_p = stack(outs_p, 3).reshape(depth, b, -1, 2, HEAD_DIM)
    dkv_p = stack(outs_p, 4).reshape(depth, b, t, 2, DIFF_HEADS, 2 * HEAD_DIM)
    fkv_s = stack(outs_s, 0).reshape(depth, nb, 1, 2, FOX_HEADS, HEAD_DIM)
    flf_s = stack(outs_s, 1).reshape(depth, nb, 1, FOX_HEADS)
    nkv_s = stack(outs_s, 2).reshape(depth, nb, 1, 4, HEAD_DIM)
    nwin_s = stack(outs_s, 3).reshape(depth, nb, -1, 2, HEAD_DIM)
    dkv_s = stack(outs_s, 4).reshape(depth, nb, 1, 2, DIFF_HEADS, 2 * HEAD_DIM)
    return (y_prompt, y_sample, fkv_p, flf_p, nkv_p, nwin_p, dkv_p, fkv_s, flf_s, nkv_s, nwin_s, dkv_s)
```

```python
import functools
import math

import numpy as np
import jax
import jax.numpy as jnp
from jax import lax
from jax.experimental import pallas as pl
from jax.experimental.pallas import tpu as pltpu

F32 = jnp.float32
BF16 = jnp.bfloat16

D_MODEL = 1024
HEAD_DIM = 64
FOX_HEADS = 4
NSA_HEADS = 4
NSA_BLOCK = 64
NSA_TOPK = 16
NSA_WINDOW = 512
DIFF_HEADS = 4
FOX_W = FOX_HEADS * HEAD_DIM
NSA_W = NSA_HEADS * HEAD_DIM
DIFF_W = DIFF_HEADS * 2 * HEAD_DIM
D_FF = 4 * D_MODEL
ROPE_THETA = 10000.0
RMS_EPS = 1e-6
NEG_INF = -1e30
PAGE_SIZE = 128
IN_SIZES = (FOX_W, FOX_W, FOX_W, FOX_HEADS,
            NSA_W, HEAD_DIM, HEAD_DIM, HEAD_DIM, HEAD_DIM, HEAD_DIM, HEAD_DIM, 3 * NSA_HEADS,
            DIFF_W, DIFF_W, DIFF_W)
SCALE = HEAD_DIM ** -0.5
LOG2E = 1.4426950408889634
QSCALE = SCALE * LOG2E
LANES = 128
SUBLANES = 8
VMEM_LIMIT = 56 * 1024 * 1024

C_FQ, C_FKV, C_NQ, C_NSA, C_WIN, C_DQ, C_DK, C_DV, C_SMALL, C_END = (
    0, 256, 768, 1024, 1280, 1408, 1920, 2432, 2944, 3072)


def _dot(a, b):
    return jnp.dot(a, b, preferred_element_type=F32)


def _dot_nt(a, b):
    return lax.dot_general(a, b, (((1,), (1,)), ((), ())), preferred_element_type=F32)


def _bf(x):
    return x.astype(BF16)


def _dot_split3(a, b):
    hi = _bf(a)
    r1 = a - hi.astype(F32)
    mid = _bf(r1)
    lo = _bf(r1 - mid.astype(F32))
    return _dot(hi, b) + _dot(mid, b) + _dot(lo, b)


def _rms(x, g):
    return x * lax.rsqrt(jnp.mean(x * x, axis=-1, keepdims=True) + RMS_EPS) * g


def _params(sem):
    return pltpu.CompilerParams(dimension_semantics=sem, vmem_limit_bytes=VMEM_LIMIT)


def _full(a):
    nd = a.ndim
    return pl.BlockSpec(a.shape, lambda *_: (0,) * nd)


def _rope(y, cos, sin):
    lane = lax.broadcasted_iota(jnp.int32, (1, LANES), 1)
    first_half = (lane % HEAD_DIM) < (HEAD_DIM // 2)
    outs = []
    for c in range(y.shape[1] // LANES):
        blk = y[:, c * LANES:(c + 1) * LANES]
        partner = jnp.where(first_half,
                            pltpu.roll(blk, LANES - HEAD_DIM // 2, 1),
                            pltpu.roll(blk, HEAD_DIM // 2, 1))
        outs.append(blk * cos + partner * sin)
    return outs[0] if len(outs) == 1 else jnp.concatenate(outs, axis=1)


def _in_proj_kernel(x_ref, g_ref, w_ref, b_ref, tab_ref,
                    fq_ref, fkv_ref, fkvb_ref, nq_ref, nsa_ref, nsab_ref, win_ref, winb_ref,
                    dq_ref, dkv_ref, dkvb_ref, small_ref):
    h = _bf(_rms(x_ref[...], g_ref[...]))
    seg = lambda a, b: _dot(h, w_ref[:, a:b])
    cos2, sin2 = tab_ref[:, 0:128], tab_ref[:, 128:256]
    cos1, sin1 = tab_ref[:, 256:384], tab_ref[:, 384:512]

    fq_ref[...] = _bf(seg(C_FQ, C_FKV) * QSCALE)
    y = seg(C_FKV, C_NQ)
    fkv_ref[...] = y
    fkvb_ref[...] = _bf(y)
    nq_ref[...] = _bf(_rope(seg(C_NQ, C_NSA), cos2, sin2) * QSCALE)
    y = _rope(seg(C_NSA, C_WIN), cos1, sin1)
    nsa_ref[...] = y
    nsab_ref[...] = _bf(y)
    y = _rope(seg(C_WIN, C_DQ), cos1, sin1)
    win_ref[...] = y
    winb_ref[...] = _bf(y)
    dq_ref[...] = _bf(_rope(seg(C_DQ, C_DK), cos2, sin2) * QSCALE)
    y = _rope(seg(C_DK, C_DV), cos2, sin2)
    dkv_ref[:, 0:DIFF_W] = y
    dkvb_ref[:, 0:DIFF_W] = _bf(y)
    y = seg(C_DV, C_SMALL)
    dkv_ref[:, DIFF_W:2 * DIFF_W] = y
    dkvb_ref[:, DIFF_W:2 * DIFF_W] = _bf(y)
    z = seg(C_SMALL, C_END) + b_ref[...]
    lane = lax.broadcasted_iota(jnp.int32, (1, LANES), 1)
    log_sig = jnp.minimum(z, 0.0) - jnp.log1p(jnp.exp(-jnp.abs(z)))
    small_ref[...] = jnp.where(lane < FOX_HEADS, log_sig, jax.nn.sigmoid(z))


def _in_proj(x, ln_g, w_re, b_small, tab, tm):
    n = x.shape[0]
    period = tab.shape[0] // tm
    row = lambda w: pl.BlockSpec((tm, w), lambda i: (i, 0))
    widths = (FOX_W, 2 * FOX_W, 2 * FOX_W, NSA_W, 256, 256, 128, 128, DIFF_W, 2 * DIFF_W, 2 * DIFF_W, LANES)
    dtypes = (BF16, F32, BF16, BF16, F32, BF16, F32, BF16, BF16, F32, BF16, F32)
    return pl.pallas_call(
        _in_proj_kernel,
        grid=(n // tm,),
        in_specs=[row(D_MODEL), _full(ln_g), _full(w_re), _full(b_small),
                  pl.BlockSpec((tm, 512), lambda i: (i % period, 0))],
        out_specs=[row(w) for w in widths],
        out_shape=[jax.ShapeDtypeStruct((n, w), d) for w, d in zip(widths, dtypes)],
        compiler_params=_params(("arbitrary",)),
        name="in_proj",
    )(x, ln_g, w_re, b_small, tab)


def _cumsum_kernel(x_ref, o_ref):
    x = x_ref[0]
    t = x.shape[1]
    idx = lax.broadcasted_iota(jnp.int32, x.shape, 1)
    shift = 1
    while shift < t:
        x = x + jnp.where(idx >= shift, pltpu.roll(x, shift, 1), 0.0)
        shift *= 2
    x = x * LOG2E
    hi = _bf(x).astype(F32)
    mid = _bf(x - hi).astype(F32)
    lo = _bf(x - hi - mid).astype(F32)
    o_ref[0, 0] = hi
    o_ref[0, 1] = mid
    o_ref[0, 2] = lo


def _cumsum_parts(x):
    b, h, t = x.shape
    x = jnp.pad(x, ((0, 0), (0, SUBLANES - h), (0, 0)))
    return pl.pallas_call(
        _cumsum_kernel, grid=(b,),
        in_specs=[pl.BlockSpec((1, SUBLANES, t), lambda i: (i, 0, 0))],
        out_specs=pl.BlockSpec((1, 3, SUBLANES, t), lambda i: (i, 0, 0, 0)),
        out_shape=jax.ShapeDtypeStruct((b, 3, SUBLANES, t), F32),
        compiler_params=_params(("arbitrary",)), name="fox_cumsum")(x)[:, :, :h]


def _diff_lambda(dl_ref, li_ref):
    dl = dl_ref[...]
    a = jnp.sum(dl[0:1] * dl[1:2], axis=-1, keepdims=True)
    b = jnp.sum(dl[2:3] * dl[3:4], axis=-1, keepdims=True)
    return jnp.exp(a) - jnp.exp(b) + li_ref[0:1, 0:1]


def _compress_kernel(x_ref, pe_ref, phi_ref, phit_ref, o_ref, ot_ref):
    x = x_ref[0]
    nb = x.shape[0] // NSA_BLOCK
    mean = jnp.sum(x.reshape(nb, NSA_BLOCK, LANES), axis=1) * (1.0 / NSA_BLOCK)
    pe_mean = jnp.sum(pe_ref[0], axis=0, keepdims=True) * (1.0 / NSA_BLOCK)
    mp = _bf(mean + pe_mean)
    o_ref[0] = _dot(mp, phi_ref[0])
    ot_ref[0] = _dot_nt(phit_ref[0], mp)


def _compress_prompt(x, pe_cat, phi_bd, phi_bdt):
    b, t, _ = x.shape
    nb = t // NSA_BLOCK
    return pl.pallas_call(
        _compress_kernel,
        grid=(b,),
        in_specs=[pl.BlockSpec((1, t, LANES), lambda i: (i, 0, 0)), _full(pe_cat), _full(phi_bd), _full(phi_bdt)],
        out_specs=[pl.BlockSpec((1, nb, LANES), lambda i: (i, 0, 0)),
                   pl.BlockSpec((1, LANES, nb), lambda i: (i, 0, 0))],
        out_shape=[jax.ShapeDtypeStruct((b, nb, LANES), F32), jax.ShapeDtypeStruct((b, LANES, nb), F32)],
        compiler_params=_params(("arbitrary",)),
        name="nsa_compress",
    )(x, pe_cat, phi_bd, phi_bdt)


def _compress_pages_kernel(x_ref, pe_ref, phi_ref, o_ref, mean_scr):
    per_page = PAGE_SIZE // NSA_BLOCK
    for pg in range(x_ref.shape[0]):
        rows = x_ref[pg].T
        mean_scr[pg * per_page:(pg + 1) * per_page, :] = (
            jnp.sum(rows.reshape(per_page, NSA_BLOCK, LANES), axis=1) * (1.0 / NSA_BLOCK))
    pe_mean = jnp.sum(pe_ref[0], axis=0, keepdims=True) * (1.0 / NSA_BLOCK)
    o_ref[0] = _dot(_bf(mean_scr[...] + pe_mean), phi_ref[0])


def _compress_pages(x, pe_cat, phi_bd):
    nl, n_pool, _, page = x.shape
    per_page = page // NSA_BLOCK
    pb = 32
    while n_pool % pb:
        pb //= 2
    assert pb * per_page % SUBLANES == 0
    return pl.pallas_call(
        _compress_pages_kernel,
        grid=(nl, n_pool // pb),
        in_specs=[pl.BlockSpec((None, pb, 2 * HEAD_DIM, page), lambda l, i: (l, i, 0, 0)),
                  pl.BlockSpec((1, NSA_BLOCK, LANES), lambda l, i: (l, 0, 0)),
                  pl.BlockSpec((1, LANES, LANES), lambda l, i: (l, 0, 0))],
        out_specs=pl.BlockSpec((1, pb * per_page, LANES), lambda l, i: (l, i, 0)),
        out_shape=jax.ShapeDtypeStruct((nl, n_pool * per_page, LANES), F32),
        scratch_shapes=[pltpu.VMEM((pb * per_page, LANES), F32)],
        compiler_params=_params(("arbitrary", "arbitrary")),
        name="nsa_compress_pages",
    )(x, pe_cat, phi_bd)


def _topk_membership(score, n_valid, n_sel):
    blk = lax.broadcasted_iota(jnp.int32, score.shape, 1)
    rank = jnp.zeros(score.shape, F32)
    for n in range(n_valid):
        col = score[:, n:n + 1]
        beats = (col > score) | ((col == score) & (blk > n))
        rank = rank + beats.astype(F32)
    return ((rank < n_sel) & (blk < n_valid)).astype(F32)


V_ONES_PAD = 16


def _online_update(s, vt_b, m, acc):
    m_new = jnp.maximum(m, jnp.max(s, axis=0, keepdims=True))
    p = jnp.exp2(s - m_new)
    return m_new, jnp.exp2(m - m_new) * acc + _dot(vt_b, _bf(p))


def _online_init(rows, cols):
    return jnp.full((1, cols), NEG_INF, F32), jnp.zeros((rows, cols), F32)


def _online_update_split(s, vts, m, *accs):
    w = s.shape[1] // len(vts)
    m_new = jnp.maximum(m, jnp.max(s, axis=0, keepdims=True))
    p = _bf(jnp.exp2(s - m_new))
    alpha = jnp.exp2(m - m_new)
    new = [alpha[:, k * w:(k + 1) * w] * acc + _dot(vt, p[:, k * w:(k + 1) * w])
           for k, (vt, acc) in enumerate(zip(vts, accs))]
    return (m_new, *new)


def _online_init_split(rows, cols, n):
    return (jnp.full((1, n * cols), NEG_INF, F32),) + (jnp.zeros((rows, cols), F32),) * n


def _block_diag(a, b):
    z = jnp.zeros_like(a)
    return jnp.concatenate([jnp.concatenate([a, z], axis=1), jnp.concatenate([z, b], axis=1)], axis=0)


def _fox_t_kernel(qw_ref, k_ref, vt_ref, o_ref, *, tq):
    i = pl.program_id(1)
    va = HEAD_DIM + V_ONES_PAD
    groups = FOX_HEADS // 2
    wq = [_block_diag(qw_ref[0, (2 * g) * LANES:(2 * g + 1) * LANES, :],
                      qw_ref[0, (2 * g + 1) * LANES:(2 * g + 2) * LANES, :]) for g in range(groups)]
    qpos = i * tq + lax.broadcasted_iota(jnp.int32, (1, 2 * tq), 1) % tq
    krow = lax.broadcasted_iota(jnp.int32, (tq, 1), 0)

    def step(j, carry, diag=False):
        rows = pl.ds(pl.multiple_of(j * tq, tq), tq)
        out = []
        for g in range(groups):
            s = _dot(k_ref[0, rows, 2 * g * LANES:(2 * g + 2) * LANES], wq[g])
            if diag:
                s = jnp.where(j * tq + krow <= qpos, s, NEG_INF)
            vts = [vt_ref[0, j, (2 * g + k) * va:(2 * g + k + 1) * va, :] for k in range(2)]
            out.extend(_online_update_split(s, vts, *carry[3 * g:3 * g + 3]))
        return tuple(out)

    carry = lax.fori_loop(0, i, step, _online_init_split(va, tq, 2) * groups)
    carry = step(i, carry, diag=True)
    accs = [carry[3 * g + 1 + k] for g in range(groups) for k in range(2)]
    o_ref[0] = jnp.concatenate([a[0:HEAD_DIM] / a[HEAD_DIM:HEAD_DIM + 1] for a in accs], axis=0).T


def _fox_prompt_t(qw, kaug, vt, tq):
    b, _, t = qw.shape
    va = vt.shape[2]
    return pl.pallas_call(
        functools.partial(_fox_t_kernel, tq=tq),
        grid=(b, t // tq),
        in_specs=[pl.BlockSpec((1, FOX_HEADS * LANES, tq), lambda bi, i: (bi, 0, i)),
                  pl.BlockSpec((1, t, FOX_HEADS * LANES), lambda bi, i: (bi, 0, 0)),
                  pl.BlockSpec((1, t // tq, va, tq), lambda bi, i: (bi, 0, 0, 0))],
        out_specs=pl.BlockSpec((1, tq, FOX_W), lambda bi, i: (bi, i, 0)),
        out_shape=jax.ShapeDtypeStruct((b, t, FOX_W), F32),
        compiler_params=_params(("arbitrary", "arbitrary")),
        name="fox_prompt",
    )(qw, kaug, vt)


def _diff_t_kernel(qt_ref, k_ref, vt_ref, dl_ref, li_ref, o_ref, *, tq):
    i = pl.program_id(1)
    hw = 2 * HEAD_DIM
    va = hw + V_ONES_PAD
    lam = _diff_lambda(dl_ref, li_ref)
    drow = lax.broadcasted_iota(jnp.int32, (hw, 1), 0)
    groups = DIFF_HEADS // 2

    def head_weights(h):
        qt = qt_ref[0, h * hw:(h + 1) * hw, :].astype(F32)
        return _bf(jnp.concatenate([jnp.where(drow < HEAD_DIM, qt, 0.0),
                                    jnp.where(drow >= HEAD_DIM, qt, 0.0)], axis=1))

    wq = [_block_diag(head_weights(2 * g), head_weights(2 * g + 1)) for g in range(groups)]
    qpos = i * tq + lax.broadcasted_iota(jnp.int32, (1, 4 * tq), 1) % tq
    krow = lax.broadcasted_iota(jnp.int32, (tq, 1), 0)

    def step(j, carry, diag=False):
        rows = pl.ds(pl.multiple_of(j * tq, tq), tq)
        out = []
        for g in range(groups):
            s = _dot(k_ref[0, rows, 2 * g * hw:(2 * g + 2) * hw], wq[g])
            if diag:
                s = jnp.where(j * tq + krow <= qpos, s, NEG_INF)
            vts = [vt_ref[0, j, (2 * g + k) * va:(2 * g + k + 1) * va, :] for k in range(2)]
            out.extend(_online_update_split(s, vts, *carry[3 * g:3 * g + 3]))
        return tuple(out)

    carry = lax.fori_loop(0, i, step, _online_init_split(va, 2 * tq, 2) * groups)
    carry = step(i, carry, diag=True)
    outs = []
    for acc in [carry[3 * g + 1 + k] for g in range(groups) for k in range(2)]:
        o = acc[0:hw] / acc[hw:hw + 1]
        outs.append(o[:, 0:tq] - lam * o[:, tq:2 * tq])
    o_ref[0] = jnp.concatenate(outs, axis=0).T


def _diff_prompt_t(qt, k_b, vt, dl, li, tq):
    b, _, t = qt.shape
    va = vt.shape[2]
    return pl.pallas_call(
        functools.partial(_diff_t_kernel, tq=tq),
        grid=(b, t // tq),
        in_specs=[pl.BlockSpec((1, DIFF_W, tq), lambda bi, i: (bi, 0, i)),
                  pl.BlockSpec((1, t, DIFF_W), lambda bi, i: (bi, 0, 0)),
                  pl.BlockSpec((1, t // tq, va, tq), lambda bi, i: (bi, 0, 0, 0)),
                  _full(dl), _full(li)],
        out_specs=pl.BlockSpec((1, tq, DIFF_W), lambda bi, i: (bi, i, 0)),
        out_shape=jax.ShapeDtypeStruct((b, t, DIFF_W), F32),
        compiler_params=_params(("arbitrary", "arbitrary")),
        name="diff_prompt",
    )(qt, k_b, vt, dl, li)


def _topk_membership_t(score, n_sel):
    nb = score.shape[0]
    blk = lax.broadcasted_iota(jnp.int32, score.shape, 0)
    rank = jnp.zeros(score.shape, F32)
    for n in range(nb):
        row = score[n:n + 1, :]
        beats = (row > score) | ((row == score) & (blk > n))
        rank = rank + beats.astype(F32)
    return (rank < n_sel).astype(F32)


def _nsa_t_kernel(qt_ref, cb_ref, cbt_ref, ks_ref, vst_ref, kw_ref, vwt_ref, gt_ref, o_ref, *, tq, n_sel):
    i = pl.program_id(1)
    nb = cb_ref.shape[1]
    nh = NSA_HEADS
    qt4 = jnp.concatenate([qt_ref[0, h * HEAD_DIM:(h + 1) * HEAD_DIM, :] for h in range(nh)], axis=1)
    qpos = i * tq + lax.broadcasted_iota(jnp.int32, (1, tq), 1)
    qpos4 = i * tq + lax.broadcasted_iota(jnp.int32, (1, nh * tq), 1) % tq
    blk = lax.broadcasted_iota(jnp.int32, (nb, 1), 0)
    krow = lax.broadcasted_iota(jnp.int32, (tq, 1), 0)

    kcb = _bf(cb_ref[0][:, 0:HEAD_DIM])
    vcbt = _bf(cbt_ref[0][HEAD_DIM:2 * HEAD_DIM, :])
    done4 = ((blk + 1) * NSA_BLOCK - 1) <= qpos4
    s = jnp.where(done4, _dot(kcb, qt4), NEG_INF)
    e = jnp.exp2(s - jnp.max(s, axis=0, keepdims=True))
    p_c = e / jnp.sum(e, axis=0, keepdims=True) * done4.astype(F32)
    o_cmp = _dot(vcbt, _bf(p_c))
    imp = p_c[:, 0:tq]
    for h in range(1, nh):
        imp = imp + p_c[:, h * tq:(h + 1) * tq]

    cur = qpos // NSA_BLOCK
    forced = (blk == 0) | (blk == cur) | (blk == cur - 1)
    score = jnp.where(forced, jnp.inf, jnp.where(blk <= cur, imp, -jnp.inf))
    sel = _bf(_topk_membership_t(score, n_sel))
    blk_lane = lax.broadcasted_iota(jnp.int32, (1, nb), 1)
    va = HEAD_DIM + V_ONES_PAD

    def sel_step(j, carry, diag=False):
        rows = pl.ds(pl.multiple_of(j * tq, tq), tq)
        kpos = j * tq + krow
        expand = _bf((kpos // NSA_BLOCK == blk_lane).astype(F32))
        chosen = _dot(expand, sel)
        if diag:
            chosen = jnp.where(kpos <= qpos, chosen, 0.0)
        chosen4 = jnp.concatenate([chosen] * nh, axis=1)
        s = jnp.where(chosen4 > 0.5, _dot(ks_ref[0, rows, 0:HEAD_DIM], qt4), NEG_INF)
        return _online_update(s, vst_ref[0, j], *carry)

    carry = sel_step(i, _online_init(va, nh * tq), diag=True)
    _, acc = lax.fori_loop(0, i, sel_step, carry)
    o_sel = acc[0:HEAD_DIM] / acc[HEAD_DIM:HEAD_DIM + 1]

    carry = _online_init(va, nh * tq)
    for d in range(NSA_WINDOW // tq + 1):
        j = jnp.maximum(i - d, 0)
        rows = pl.ds(pl.multiple_of(j * tq, tq), tq)
        dist = qpos4 - (j * tq + krow) + jnp.where(i - d >= 0, 0, 2 * NSA_WINDOW + tq)
        ok = (dist >= 0) & (dist <= NSA_WINDOW)
        s = jnp.where(ok, _dot(kw_ref[0, rows, 0:HEAD_DIM], qt4), NEG_INF)
        carry = _online_update(s, vwt_ref[0, j], *carry)
    o_win = carry[1][0:HEAD_DIM] / carry[1][HEAD_DIM:HEAD_DIM + 1]

    gt = gt_ref[0]
    outs = []
    for h in range(nh):
        cols = slice(h * tq, (h + 1) * tq)
        r0 = FOX_HEADS + 3 * h
        outs.append(gt[r0:r0 + 1] * o_cmp[:, cols] + gt[r0 + 1:r0 + 2] * o_sel[:, cols]
                    + gt[r0 + 2:r0 + 3] * o_win[:, cols])
    o_ref[0] = jnp.concatenate(outs, axis=0).T


def _nsa_prompt_t(qt, cb, cbt, nsa_b, vst, win_b, vwt, small_t, tq):
    b, _, t = qt.shape
    nb = t // NSA_BLOCK
    va = vst.shape[2]
    chunks = pl.BlockSpec((1, t // tq, va, tq), lambda bi, i: (bi, 0, 0, 0))
    return pl.pallas_call(
        functools.partial(_nsa_t_kernel, tq=tq, n_sel=min(NSA_TOPK, nb)),
        grid=(b, t // tq),
        in_specs=[pl.BlockSpec((1, NSA_W, tq), lambda bi, i: (bi, 0, i)),
                  pl.BlockSpec((1, nb, LANES), lambda bi, i: (bi, 0, 0)),
                  pl.BlockSpec((1, LANES, nb), lambda bi, i: (bi, 0, 0)),
                  pl.BlockSpec((1, t, LANES), lambda bi, i: (bi, 0, 1)),
                  chunks,
                  pl.BlockSpec((1, t, LANES), lambda bi, i: (bi, 0, 0)),
                  chunks,
                  pl.BlockSpec((1, LANES, tq), lambda bi, i: (bi, 0, i))],
        out_specs=pl.BlockSpec((1, tq, NSA_W), lambda bi, i: (bi, i, 0)),
        out_shape=jax.ShapeDtypeStruct((b, t, NSA_W), F32),
        compiler_params=_params(("arbitrary", "arbitrary")),
        name="nsa_prompt",
    )(qt, cb, cbt, nsa_b, vst, win_b, vwt, small_t)


def _vt_chunks(v, tq, heads):
    b, t, w = v.shape
    dv = w // heads
    vt = jnp.swapaxes(v.reshape(b, t // tq, tq, heads, dv), 2, 4)
    vt = jnp.swapaxes(vt, 2, 3)
    ones = jnp.ones((b, t // tq, heads, 1, tq), v.dtype)
    pad = jnp.zeros((b, t // tq, heads, V_ONES_PAD - 1, tq), v.dtype)
    return jnp.concatenate([vt, ones, pad], axis=3).reshape(b, t // tq, heads * (dv + V_ONES_PAD), tq)


DEC_ROWS = 16


def _head_rows(x, n_heads):
    n = x.shape[0]
    return jnp.pad(x.reshape(n, n_heads, HEAD_DIM), ((0, 0), (0, DEC_ROWS - n_heads), (0, 0)))


def _dec_select_kernel(pt_ref, q_ref, cbp_ref, ocmp_ref, sel_ref, cb_scr, *, n_pages, past, n_sel):
    b = pl.program_id(0)
    n_cmp = (past + 1) // NSA_BLOCK
    n_blk = -(-(past + 1) // NSA_BLOCK)
    per_page = PAGE_SIZE // NSA_BLOCK
    cb_scr[...] = jnp.zeros(cb_scr.shape, F32)
    for p in range(n_pages):
        cb_scr[p * per_page:(p + 1) * per_page, :] = cbp_ref[pt_ref[b, p]]
    cb = cb_scr[...]
    kcb, vcb = _bf(cb[:, 0:HEAD_DIM]), _bf(cb[:, HEAD_DIM:])
    blk = lax.broadcasted_iota(jnp.int32, (1, LANES), 1)
    done = (blk < n_cmp) & (((blk + 1) * NSA_BLOCK - 1) <= past)
    s = jnp.where(done, _dot_nt(q_ref[0], kcb), NEG_INF)
    e = jnp.exp2(s - jnp.max(s, axis=-1, keepdims=True))
    p_c = e / jnp.sum(e, axis=-1, keepdims=True) * done.astype(F32)
    ocmp_ref[0] = _dot(_bf(p_c), vcb)
    head_row = lax.broadcasted_iota(jnp.int32, (DEC_ROWS, 1), 0) < NSA_HEADS
    imp = jnp.sum(jnp.where(head_row, p_c, 0.0), axis=0, keepdims=True)
    imp = jnp.where(blk < n_cmp, imp, -jnp.inf)
    cur = past // NSA_BLOCK
    forced = (blk == 0) | (blk == cur) | (blk == cur - 1)
    score = jnp.where(forced, jnp.inf, jnp.where(blk <= cur, imp, -jnp.inf))
    sel_ref[0] = _topk_membership(score, n_blk, n_sel).astype(jnp.int32)


def _dec_select(page_table, nq16, cb_pool, past):
    nb, n_pages = page_table.shape
    n_blk = -(-(past + 1) // NSA_BLOCK)
    assert n_pages * (PAGE_SIZE // NSA_BLOCK) <= LANES and n_blk <= LANES
    kern = functools.partial(_dec_select_kernel, n_pages=n_pages, past=past, n_sel=min(NSA_TOPK, n_blk))
    return pl.pallas_call(
        kern,
        grid_spec=pltpu.PrefetchScalarGridSpec(
            num_scalar_prefetch=1, grid=(nb,),
            in_specs=[pl.BlockSpec((1, DEC_ROWS, HEAD_DIM), lambda b, pt: (b, 0, 0)),
                      pl.BlockSpec(cb_pool.shape, lambda b, pt: (0, 0, 0))],
            out_specs=[pl.BlockSpec((1, DEC_ROWS, HEAD_DIM), lambda b, pt: (b, 0, 0)),
                       pl.BlockSpec((1, 1, LANES), lambda b, pt: (b, 0, 0))],
            scratch_shapes=[pltpu.VMEM((LANES, LANES), F32)]),
        out_shape=[jax.ShapeDtypeStruct((nb, DEC_ROWS, HEAD_DIM), F32),
                   jax.ShapeDtypeStruct((nb, 1, LANES), jnp.int32)],
        compiler_params=_params(("arbitrary",)),
        name="nsa_dec_select",
    )(page_table, nq16, cb_pool)


def _pick_lane_per_row(x_row, lane_of_row):
    lane = lax.broadcasted_iota(jnp.int32, (lane_of_row.shape[0], LANES), 1)
    return jnp.sum(jnp.where(lane == lane_of_row, x_row, 0.0), axis=1, keepdims=True)


def _masked_rows(x_row, n_rows, group):
    w = x_row.shape[1]
    lane = lax.broadcasted_iota(jnp.int32, (n_rows, w), 1)
    row = lax.broadcasted_iota(jnp.int32, (n_rows, w), 0)
    return _bf(jnp.where(lane // group == row, x_row.astype(F32), 0.0))


def _softmax_with_new(s, s_new):
    m = jnp.maximum(jnp.max(s, axis=-1, keepdims=True), s_new)
    p = jnp.exp2(s - m)
    p_new = jnp.exp2(s_new - m)
    return _bf(p), _bf(p_new).astype(F32), jnp.sum(p, axis=-1, keepdims=True) + p_new


def _dec_attn_kernel(pt_ref, sel_ref, *refs, n_pages, past):
    fox_refs, lf_refs, nsa_refs, diff_refs = (refs[k * n_pages:(k + 1) * n_pages] for k in range(4))
    (win_ref, fq_ref, fnew_ref, small_ref, nq_ref, nnew_ref, wnew_ref, dq_ref, dnew_ref,
     ocmp_ref, dl_ref, li_ref, ofox_ref, onsa_ref, odiff_ref) = refs[4 * n_pages:]
    b = pl.program_id(0)
    row8 = lax.broadcasted_iota(jnp.int32, (SUBLANES, 1), 0)
    row16 = lax.broadcasted_iota(jnp.int32, (DEC_ROWS, 1), 0)
    lane = lax.broadcasted_iota(jnp.int32, (1, LANES), 1)
    hw = 2 * HEAD_DIM
    pages = range(n_pages)
    chunk = lambda x, pg: x[:, pg * PAGE_SIZE:(pg + 1) * PAGE_SIZE]
    small = small_ref[0]

    fq8 = _masked_rows(fq_ref[0], SUBLANES, HEAD_DIM)
    fnew = fnew_ref[0]
    k_new, v_new = _bf(fnew[:, 0:FOX_W]).astype(F32), _bf(fnew[:, FOX_W:]).astype(F32)
    s = jnp.concatenate([_dot(fq8, _bf(fox_refs[pg][0:FOX_W, :])) for pg in pages], axis=1)
    key_j = lax.broadcasted_iota(jnp.int32, (PAGE_SIZE, PAGE_SIZE), 0)
    key_s = lax.broadcasted_iota(jnp.int32, (PAGE_SIZE, PAGE_SIZE), 1)
    after = _bf((key_j > key_s).astype(F32))
    later = _pick_lane_per_row(small, row8)
    bias = [None] * n_pages
    for pg in reversed(pages):
        lf = lf_refs[pg][...]
        bias[pg] = _dot_split3(lf, after) + later
        later = later + jnp.sum(lf, axis=1, keepdims=True)
    s = s + jnp.concatenate(bias, axis=1) * LOG2E
    s_new = jnp.sum(fq8.astype(F32) * k_new, axis=1, keepdims=True)
    p, p_new, l = _softmax_with_new(s, s_new)
    acc = p_new * v_new
    for pg in pages:
        acc = acc + _dot_nt(chunk(p, pg), _bf(fox_refs[pg][FOX_W:2 * FOX_W, :]))
    lane_f = lax.broadcasted_iota(jnp.int32, (SUBLANES, FOX_W), 1)
    ofox_ref[0] = jnp.sum(jnp.where((lane_f // HEAD_DIM) == row8, acc / l, 0.0), axis=0, keepdims=True)

    nq = nq_ref[0]
    nnew = nnew_ref[0]
    per_page = PAGE_SIZE // NSA_BLOCK
    parts = []
    for pg in pages:
        chosen = jnp.where(lane < NSA_BLOCK, sel_ref[b, pg * per_page], sel_ref[b, pg * per_page + 1]) > 0
        parts.append(jnp.where(chosen, _dot(nq, _bf(nsa_refs[pg][0:HEAD_DIM, :])), NEG_INF))
    s = jnp.concatenate(parts, axis=1)
    s_new = jnp.sum(nq.astype(F32) * _bf(nnew[:, 2 * HEAD_DIM:3 * HEAD_DIM]).astype(F32), axis=1, keepdims=True)
    p, p_new, l = _softmax_with_new(s, s_new)
    acc = p_new * _bf(nnew[:, 3 * HEAD_DIM:]).astype(F32)
    for pg in pages:
        acc = acc + _dot_nt(chunk(p, pg), _bf(nsa_refs[pg][HEAD_DIM:2 * HEAD_DIM, :]))
    o_sel = acc / l

    win = win_ref[0]
    wnew = wnew_ref[0]
    n_win = win.shape[1]
    pos = past - n_win + lax.broadcasted_iota(jnp.int32, (1, n_win), 1)
    ok = (past - pos <= NSA_WINDOW) & (pos >= 0)
    s = jnp.where(ok, _dot(nq, _bf(win[0:HEAD_DIM, :])), NEG_INF)
    s_new = jnp.sum(nq.astype(F32) * _bf(wnew[:, 0:HEAD_DIM]).astype(F32), axis=1, keepdims=True)
    p, p_new, l = _softmax_with_new(s, s_new)
    o_win = (_dot_nt(p, _bf(win[HEAD_DIM:, :])) + p_new * _bf(wnew[:, HEAD_DIM:]).astype(F32)) / l
    g = [_pick_lane_per_row(small, FOX_HEADS + 3 * row16 + c) for c in range(3)]
    onsa_ref[0] = g[0] * ocmp_ref[0] + g[1] * o_sel + g[2] * o_win

    lam = _diff_lambda(dl_ref, li_ref)
    dq8 = _masked_rows(dq_ref[0], SUBLANES, HEAD_DIM)
    dnew = dnew_ref[0]

    def page_rows(pg, first):
        return _bf(jnp.concatenate(
            [diff_refs[pg][pl.ds(first + h, PAGE_SIZE, stride=2 * DIFF_HEADS), :] for h in range(DIFF_HEADS)],
            axis=1))

    s = jnp.concatenate([_dot_nt(dq8, page_rows(pg, 0)) for pg in pages], axis=1)
    s_new = jnp.sum(dq8.astype(F32) * _bf(dnew[:, 0:DIFF_W]).astype(F32), axis=1, keepdims=True)
    p, p_new, l = _softmax_with_new(s, s_new)
    acc = p_new * _bf(dnew[:, DIFF_W:]).astype(F32)
    for pg in pages:
        acc = acc + _dot(chunk(p, pg), page_rows(pg, DIFF_HEADS))
    lane_d = lax.broadcasted_iota(jnp.int32, (SUBLANES, DIFF_W), 1)
    signed = acc * (jnp.where(row8 % 2 == 0, 1.0, -lam) / l)
    odiff_ref[0] = jnp.sum(jnp.where((lane_d // hw) == (row8 // 2), signed, 0.0), axis=0, keepdims=True)


def _dec_attn(layer, page_table, sel, caches, win_state, new, o_cmp, dl, li, past):
    nb, n_pages = page_table.shape
    fox_kv, fox_lf, nsa_kv, diff_kv = caches
    fq_b, fkv, small, nq16, nsa_new, win_new, dq_b, dkv = new
    n_win = win_state.shape[3]
    per_seq = lambda w: pl.BlockSpec((1, 1, w), lambda b, pt, sl: (b, 0, 0))
    head_rows = pl.BlockSpec((1, DEC_ROWS, HEAD_DIM), lambda b, pt, sl: (b, 0, 0))
    r3 = lambda a: a.reshape(nb, 1, a.shape[-1])

    def paged(rows, row_block):
        return [pl.BlockSpec((None, None, rows, LANES),
                             lambda b, pt, sl, pg=pg: (layer, pt[b, pg], row_block, 0)) for pg in range(n_pages)]

    kern = functools.partial(_dec_attn_kernel, n_pages=n_pages, past=past)
    return pl.pallas_call(
        kern,
        grid_spec=pltpu.PrefetchScalarGridSpec(
            num_scalar_prefetch=2, grid=(nb,),
            in_specs=(paged(2 * FOX_W, 0) + paged(SUBLANES, 0) + paged(2 * HEAD_DIM, 1)
                      + paged(PAGE_SIZE * 2 * DIFF_HEADS, 0) + [
                pl.BlockSpec((None, 1, 2 * HEAD_DIM, n_win), lambda b, pt, sl: (layer, b, 0, 0)),
                per_seq(FOX_W), per_seq(2 * FOX_W), per_seq(LANES), head_rows, per_seq(256),
                per_seq(LANES), per_seq(DIFF_W), per_seq(2 * DIFF_W), head_rows,
                pl.BlockSpec(dl.shape, lambda b, pt, sl: (0, 0)),
                pl.BlockSpec(li.shape, lambda b, pt, sl: (0, 0)),
            ]),
            out_specs=[per_seq(FOX_W), head_rows, per_seq(DIFF_W)]),
        out_shape=[jax.ShapeDtypeStruct((nb, 1, FOX_W), F32),
                   jax.ShapeDtypeStruct((nb, DEC_ROWS, HEAD_DIM), F32),
                   jax.ShapeDtypeStruct((nb, 1, DIFF_W), F32)],
        compiler_params=_params(("arbitrary",)),
        name="dec_attn",
    )(page_table, sel, *([fox_kv] * n_pages + [fox_lf] * n_pages + [nsa_kv] * n_pages + [diff_kv] * n_pages),
      win_state, r3(fq_b), r3(fkv), r3(small), nq16, r3(nsa_new), r3(win_new), r3(dq_b), r3(dkv),
      o_cmp, dl, li)


def _post_kernel(x_ref, g1_ref, of_ref, on_ref, od_ref, gs_ref, wuf_ref, wun_ref, wud_ref,
                 wg_ref, bg_ref, wo_ref, o_ref):
    x = x_ref[...]
    h = _bf(_rms(x, g1_ref[...]))
    hw = 2 * HEAD_DIM
    od = od_ref[...]
    od = jnp.concatenate([_rms(od[:, k * hw:(k + 1) * hw], gs_ref[...]) for k in range(DIFF_HEADS)], axis=1)
    ups = (_dot(_bf(of_ref[...]), wuf_ref[...]), _dot(_bf(on_ref[...]), wun_ref[...]), _dot(_bf(od), wud_ref[...]))
    merged = None
    for k, up in enumerate(ups):
        cols = slice(k * D_MODEL, (k + 1) * D_MODEL)
        gate = jax.nn.sigmoid(_dot(h, wg_ref[:, cols]) + bg_ref[:, cols])
        merged = gate * up if merged is None else merged + gate * up
    o_ref[...] = x + _dot(_bf(merged), wo_ref[...])


def _post(x, ln1_g, o_fox, o_nsa, o_diff, gs, w_up_fox, w_up_nsa, w_up_diff, w_gate, b_gate, w_o, tm):
    n = x.shape[0]
    row = lambda w: pl.BlockSpec((tm, w), lambda i: (i, 0))
    consts = (gs, w_up_fox, w_up_nsa, w_up_diff, w_gate, b_gate, w_o)
    return pl.pallas_call(
        _post_kernel,
        grid=(n // tm,),
        in_specs=[row(D_MODEL), _full(ln1_g), row(FOX_W), row(NSA_W), row(DIFF_W)] + [_full(a) for a in consts],
        out_specs=row(D_MODEL),
        out_shape=jax.ShapeDtypeStruct((n, D_MODEL), F32),
        compiler_params=_params(("arbitrary",)),
        name="merge_out",
    )(x, ln1_g, o_fox, o_nsa, o_diff, *consts)


def _mlp_kernel(x_ref, pe_ref, g2_ref, w1_ref, w2_ref, gp_ref, wpg_ref, wp_ref, o_ref, *, ff_chunk):
    x = x_ref[...]
    h = _bf(_rms(x, g2_ref[...]))
    acc = None
    for c in range(D_FF // ff_chunk):
        cols = slice(c * ff_chunk, (c + 1) * ff_chunk)
        a = jnp.square(jnp.maximum(_dot(h, w1_ref[:, cols]), 0.0))
        part = _dot(_bf(a), w2_ref[cols, :])
        acc = part if acc is None else acc + part
    x = x + acc
    gate = jax.nn.sigmoid(_dot(_bf(_rms(x, gp_ref[...])), wpg_ref[...]))
    o_ref[...] = x + gate * _dot(_bf(pe_ref[...]), wp_ref[...])


def _mlp_ple(x, pe, ln2_g, w_ff1, w_ff2, ln_ple_g, w_ple_gate, w_ple, tm):
    n = x.shape[0]
    row = lambda w: pl.BlockSpec((tm, w), lambda i: (i, 0))
    consts = (ln2_g, w_ff1, w_ff2, ln_ple_g, w_ple_gate, w_ple)
    return pl.pallas_call(
        functools.partial(_mlp_kernel, ff_chunk=1024),
        grid=(n // tm,),
        in_specs=[row(D_MODEL), row(pe.shape[1])] + [_full(a) for a in consts],
        out_specs=row(D_MODEL),
        out_shape=jax.ShapeDtypeStruct((n, D_MODEL), F32),
        compiler_params=_params(("arbitrary",)),
        name="mlp_ple",
    )(x, pe, *consts)


def _norm_kernel(x_ref, g_ref, o_ref):
    o_ref[...] = _rms(x_ref[...], g_ref[...])


def _final_norm(x, g, tm):
    n = x.shape[0]
    row = pl.BlockSpec((tm, D_MODEL), lambda i: (i, 0))
    return pl.pallas_call(
        _norm_kernel, grid=(n // tm,), in_specs=[row, _full(g)], out_specs=row,
        out_shape=jax.ShapeDtypeStruct((n, D_MODEL), F32),
        compiler_params=_params(("arbitrary",)), name="final_norm")(x, g)


def _rope_table(pos):
    half = HEAD_DIM // 2
    inv = ROPE_THETA ** (-jnp.arange(half, dtype=F32) / half)
    ang = pos.astype(F32)[:, None] * inv[None, :]
    cos, sin = jnp.cos(ang), jnp.sin(ang)
    cos_h = jnp.concatenate([cos, cos], axis=1)
    sin_h = jnp.concatenate([-sin, sin], axis=1)
    one, zero = jnp.ones_like(cos_h), jnp.zeros_like(cos_h)
    return jnp.concatenate([cos_h, cos_h, sin_h, sin_h, cos_h, one, sin_h, zero], axis=1)


def _prep_weights(p):
    segs = jnp.split(p["w_in"], np.cumsum(IN_SIZES)[:-1].tolist(), axis=-1)
    (fq, fk, fv, ff, nq, nkc, nvc, nks, nvs, nkw, nvw, ng, dq, dk, dv) = segs
    pad = jnp.zeros(ff.shape[:-1] + (LANES - FOX_HEADS - 3 * NSA_HEADS,), F32)
    w_re = _bf(jnp.concatenate([fq, fk, fv, nq, nkc, nvc, nks, nvs, nkw, nvw, dq, dk, dv, ff, ng, pad], axis=-1))
    depth = p["w_in"].shape[0]
    b_small = jnp.concatenate([p["b_fox_f"], p["b_nsa_gate"],
                               jnp.zeros((depth, LANES - FOX_HEADS - 3 * NSA_HEADS), F32)], axis=-1)
    pe_cat = jnp.concatenate([p["nsa_pe"][:, 0], p["nsa_pe"][:, 1]], axis=-1)
    z = jnp.zeros_like(p["nsa_phi"][:, 0])
    phi_bd = _bf(jnp.concatenate([jnp.concatenate([p["nsa_phi"][:, 0], z], axis=-1),
                                  jnp.concatenate([z, p["nsa_phi"][:, 1]], axis=-1)], axis=-2))
    lam_init = jnp.asarray([0.8 - 0.6 * math.exp(-0.3 * i) for i in range(depth)], F32)
    out = dict(w_re=w_re, b_small=b_small[:, None, :], pe_cat=pe_cat, phi_bd=phi_bd,
               phi_bdt=jnp.swapaxes(phi_bd, 1, 2),
               lam_init=jnp.broadcast_to(lam_init[:, None, None], (depth, 1, LANES)),
               subln_g=(p["diff_subln_g"] * (1.0 - lam_init)[:, None])[:, None, :],
               b_merge_gate=p["b_merge_gate"][:, None, :], diff_lambda=p["diff_lambda"])
    for name in ("w_up_fox", "w_up_nsa", "w_up_diff", "w_merge_gate", "w_o", "w_ff1", "w_ff2", "w_ple", "w_ple_gate"):
        out[name] = _bf(p[name])
    for name in ("ln1_g", "ln2_g", "ln_ple_g"):
        out[name] = p[name][:, None, :]
    return out


def _dense_tail(x, pe, attn, w, i, tm):
    o_fox, o_nsa, o_diff = attn
    x = _post(x, w["ln1_g"][i], o_fox, o_nsa, o_diff, w["subln_g"][i], w["w_up_fox"][i], w["w_up_nsa"][i],
              w["w_up_diff"][i], w["w_merge_gate"][i], w["b_merge_gate"][i], w["w_o"][i], tm)
    return _mlp_ple(x, pe, w["ln2_g"][i], w["w_ff1"][i], w["w_ff2"][i], w["ln_ple_g"][i],
                    w["w_ple_gate"][i], w["w_ple"][i], tm)


def _prompt_layer(x, pe, w, i, tab, b, t, tm, tq):
    fq_b, fkv, fkv_b, nq_b, nsa_new, nsa_b, win, win_b, dq_b, dkv, dkv_b, small = _in_proj(
        x, w["ln1_g"][i], w["w_re"][i], w["b_small"][i], tab, tm)
    r = lambda a: a.reshape(b, t, a.shape[-1])
    logf = small[:, 0:FOX_HEADS]

    parts = jnp.transpose(_bf(_cumsum_parts(jnp.swapaxes(r(logf), 1, 2))), (0, 3, 2, 1))
    ones3 = jnp.ones((b, t, FOX_HEADS, 3), BF16)
    zpad = jnp.zeros((b, t, FOX_HEADS, LANES - HEAD_DIM - 6), BF16)
    fk = r(fkv_b)[:, :, 0:FOX_W].reshape(b, t, FOX_HEADS, HEAD_DIM)
    fq = r(fq_b).reshape(b, t, FOX_HEADS, HEAD_DIM)
    kaug = jnp.concatenate([fk, -parts, ones3, zpad], axis=-1).reshape(b, t, FOX_HEADS * LANES)
    qaug = jnp.concatenate([fq, ones3, parts, zpad], axis=-1)
    qw = jnp.transpose(qaug, (0, 2, 3, 1)).reshape(b, FOX_HEADS * LANES, t)
    o_fox = _fox_prompt_t(qw, kaug, _vt_chunks(r(fkv_b)[:, :, FOX_W:], tq, FOX_HEADS), tq)

    cb, cbt = _compress_prompt(r(nsa_new), w["pe_cat"][i:i + 1], w["phi_bd"][i:i + 1], w["phi_bdt"][i:i + 1])
    o_nsa = _nsa_prompt_t(jnp.swapaxes(r(nq_b), 1, 2), cb, cbt,
                          r(nsa_b), _vt_chunks(r(nsa_b)[:, :, 3 * HEAD_DIM:], tq, 1),
                          r(win_b), _vt_chunks(r(win_b)[:, :, HEAD_DIM:], tq, 1),
                          jnp.swapaxes(r(small), 1, 2), tq)
    o_diff = _diff_prompt_t(jnp.swapaxes(r(dq_b), 1, 2), r(dkv_b),
                            _vt_chunks(r(dkv_b)[:, :, DIFF_W:], tq, DIFF_HEADS),
                            w["diff_lambda"][i], w["lam_init"][i], tq)
    flat = lambda a: a.reshape(b * t, a.shape[-1])
    x = _dense_tail(x, pe, (flat(o_fox), flat(o_nsa), flat(o_diff)), w, i, tm)
    return x, (r(fkv), r(logf), r(nsa_new), r(win)[:, t - min(NSA_WINDOW, t):], r(dkv))


def _cache_views(cache_fox_kv, cache_fox_logf, cache_nsa_kv, cache_diff_kv, state_nsa_win):
    depth, n_pool, page = cache_fox_kv.shape[:3]
    nb, n_win = state_nsa_win.shape[1:3]
    fox_kv = jnp.transpose(cache_fox_kv, (0, 1, 3, 4, 5, 2)).reshape(depth, n_pool, 2 * FOX_W, page)
    fox_lf = jnp.pad(jnp.swapaxes(cache_fox_logf, 2, 3), ((0, 0), (0, 0), (0, SUBLANES - FOX_HEADS), (0, 0)))
    nsa_kv = jnp.transpose(cache_nsa_kv, (0, 1, 3, 4, 2)).reshape(depth, n_pool, 4 * HEAD_DIM, page)
    diff_kv = cache_diff_kv.reshape(depth, n_pool, page * 2 * DIFF_HEADS, 2 * HEAD_DIM)
    win_state = jnp.transpose(state_nsa_win, (0, 1, 3, 4, 2)).reshape(depth, nb, 2 * HEAD_DIM, n_win)
    win_rows = state_nsa_win.reshape(depth, nb, n_win, 2 * HEAD_DIM)
    return (fox_kv, fox_lf, nsa_kv, diff_kv), win_state, win_rows


def _sample_layer(x, pe, w, i, tab, caches, cb_pool, win_state, win_rows, page_table, past):
    nb = x.shape[0]
    new = _in_proj(x, w["ln1_g"][i], w["w_re"][i], w["b_small"][i], tab, nb)
    fq_b, fkv, fkv_b, nq_b, nsa_new, nsa_b, win, win_b, dq_b, dkv, dkv_b, small = new
    nq16 = _head_rows(nq_b, NSA_HEADS)
    o_cmp, sel = _dec_select(page_table, nq16, cb_pool[i], past)
    o_fox, o_nsa, o_diff = _dec_attn(
        i, page_table, sel.reshape(nb, LANES), caches, win_state,
        (fq_b, fkv, small, nq16, nsa_new, win, dq_b, dkv), o_cmp, w["diff_lambda"][i], w["lam_init"][i], past)
    o_nsa = o_nsa[:, 0:NSA_HEADS].reshape(nb, NSA_W)
    x = _dense_tail(x, pe, (o_fox[:, 0], o_nsa, o_diff[:, 0]), w, i, nb)
    n_win = win_rows.shape[2]
    keep = min(NSA_WINDOW, past + 1)
    new_win = jnp.concatenate([win_rows[i][:, n_win + 1 - keep:], win[:, None, :]], axis=1)
    return x, (fkv, small[:, 0:FOX_HEADS], nsa_new, new_win, dkv)


def kernel(x_prompt, x_sample, cache_fox_kv, cache_fox_logf, cache_nsa_kv, state_nsa_win, cache_diff_kv,
           page_table, p_prompt, p_sample,
           ln1_g, w_in, b_fox_f, b_nsa_gate, nsa_pe, nsa_phi, diff_lambda, diff_subln_g,
           w_up_fox, w_up_nsa, w_up_diff, w_merge_gate, b_merge_gate, w_o,
           ln2_g, w_ff1, w_ff2, w_ple, ln_ple_g, w_ple_gate, final_norm_g):
    b, t, d = x_prompt.shape
    nb, ds, _ = x_sample.shape
    depth, n_pool, page, _, _, _ = cache_fox_kv.shape
    n_pages = page_table.shape[1]
    past = n_pages * page
    n_win = state_nsa_win.shape[2]
    assert ds == 1 and page == PAGE_SIZE and n_win == min(NSA_WINDOW, past)
    tm = min(512, b * t)
    tq = min(512, t)

    w = _prep_weights(dict(
        w_in=w_in, b_fox_f=b_fox_f, b_nsa_gate=b_nsa_gate, nsa_pe=nsa_pe, nsa_phi=nsa_phi,
        diff_lambda=diff_lambda, diff_subln_g=diff_subln_g, w_up_fox=w_up_fox, w_up_nsa=w_up_nsa,
        w_up_diff=w_up_diff, w_merge_gate=w_merge_gate, b_merge_gate=b_merge_gate, w_o=w_o,
        ln1_g=ln1_g, ln2_g=ln2_g, w_ff1=w_ff1, w_ff2=w_ff2, w_ple=w_ple, ln_ple_g=ln_ple_g,
        w_ple_gate=w_ple_gate))
    tab_p = _rope_table(jnp.arange(t, dtype=jnp.int32))
    tab_s = _rope_table(jnp.full((nb,), past, jnp.int32))

    caches, win_state, win_rows = _cache_views(cache_fox_kv, cache_fox_logf, cache_nsa_kv, cache_diff_kv,
                                               state_nsa_win)
    cb_pool = _compress_pages(caches[2], w["pe_cat"], w["phi_bd"])
    cb_pool = cb_pool.reshape(depth, n_pool, page // NSA_BLOCK, LANES)

    xp = x_prompt.reshape(b * t, d)
    xs = x_sample.reshape(nb, d)
    outs_p, outs_s = [], []
    for i in range(depth):
        xp, new_p = _prompt_layer(xp, p_prompt[i].reshape(b * t, -1), w, i, tab_p, b, t, tm, tq)
        outs_p.append(new_p)
        xs, new_s = _sample_layer(xs, p_sample[i].reshape(nb, -1), w, i, tab_s,
                                  caches, cb_pool, win_state, win_rows, page_table, past)
        outs_s.append(new_s)
    fn = final_norm_g[None, :]
    y_prompt = _final_norm(xp, fn, tm).reshape(b, t, d)
    y_sample = _final_norm(xs, fn, nb).reshape(nb, 1, d)

    stack = lambda outs, k: jnp.stack([o[k] for o in outs])
    fkv_p = stack(outs_p, 0).reshape(depth, b, t, 2, FOX_HEADS, HEAD_DIM)
    flf_p = stack(outs_p, 1)
    nkv_p = stack(outs_p, 2).reshape(depth, b, t, 4, HEAD_DIM)
    nwin_p = stack(outs_p, 3).reshape(depth, b, -1, 2, HEAD_DIM)
    dkv_p = stack(outs_p, 4).reshape(depth, b, t, 2, DIFF_HEADS, 2 * HEAD_DIM)
    fkv_s = stack(outs_s, 0).reshape(depth, nb, 1, 2, FOX_HEADS, HEAD_DIM)
    flf_s = stack(outs_s, 1).reshape(depth, nb, 1, FOX_HEADS)
    nkv_s = stack(outs_s, 2).reshape(depth, nb, 1, 4, HEAD_DIM)
    nwin_s = stack(outs_s, 3).reshape(depth, nb, -1, 2, HEAD_DIM)
    dkv_s = stack(outs_s, 4).reshape(depth, nb, 1, 2, DIFF_HEADS, 2 * HEAD_DIM)
    return (y_prompt, y_sample, fkv_p, flf_p, nkv_p, nwin_p, dkv_p, fkv_s, flf_s, nkv_s, nwin_s, dkv_s)
```
